```python
import jax, jax.numpy as jnp
from jax import lax
import numpy as np

D_MODEL = 1024
BATCH = 1
SEQ = 16384
DEPTH = 4

CHUNK = 64
Q_BLOCK = 128
SGU_CHUNK = 128
HEAD_DIM = 64
D_ATTN = D_MODEL // 2
N_ATTN_HEADS = D_ATTN // HEAD_DIM
D_SGU = D_MODEL // 2
SGU_GROUP_DIM = 64
N_SGU_GROUPS = D_SGU // SGU_GROUP_DIM
D_MIX = D_ATTN + D_SGU
D_IN = 3 * D_ATTN + N_ATTN_HEADS + 2 * D_SGU
D_FF = -(-8 * D_MODEL // (3 * 256)) * 256
EPS = 1e-6

kernel_name = 'fox_gmlp_hybrid_trunk'


def rms_norm(x, g):
    xf = x.astype(jnp.float32)
    y = xf * lax.rsqrt(jnp.mean(xf * xf, axis=-1, keepdims=True) + EPS)
    return (y * g.astype(jnp.float32)).astype(x.dtype)


def layer_norm(x, g, b):
    xf = x.astype(jnp.float32)
    mu = jnp.mean(xf, axis=-1, keepdims=True)
    xc = xf - mu
    y = xc * lax.rsqrt(jnp.mean(xc * xc, axis=-1, keepdims=True) + EPS)
    return (y * g.astype(jnp.float32) + b.astype(jnp.float32)).astype(x.dtype)


def forgetting_attention(q, k, v, log_f):
    B, S, H, Dh = q.shape
    nb = S // Q_BLOCK
    c_bhs = jnp.cumsum(log_f, axis=1).transpose(0, 2, 1)
    q_blocks = q.reshape(B, nb, Q_BLOCK, H, Dh).transpose(1, 0, 2, 3, 4)
    cq_blocks = c_bhs.reshape(B, H, nb, Q_BLOCK).transpose(2, 0, 1, 3)
    starts = jnp.arange(nb, dtype=jnp.int32) * Q_BLOCK
    key_pos = jnp.arange(S, dtype=jnp.int32)
    scale = Dh ** -0.5

    def one_block(args):
        qb, cqb, start = args
        s = jnp.einsum('bqhd,bkhd->bhqk', qb, k).astype(jnp.float32) * scale
        s = s + cqb[..., :, None] - c_bhs[..., None, :]
        q_pos = start + jnp.arange(Q_BLOCK, dtype=jnp.int32)
        causal = key_pos[None, :] <= q_pos[:, None]
        s = jnp.where(causal, s, -jnp.inf)
        p = jax.nn.softmax(s, axis=-1).astype(v.dtype)
        return jnp.einsum('bhqk,bkhd->bqhd', p, v)

    out = lax.map(one_block, (q_blocks, cq_blocks, starts))
    return out.transpose(1, 0, 2, 3, 4).reshape(B, S, H * Dh)


def spatial_gating(z, ln_g, ln_b, w_s, b_s):
    B, S, _ = z.shape
    zu, zv = jnp.split(z, 2, axis=-1)
    zv = layer_norm(zv, ln_g, ln_b)
    nc = S // SGU_CHUNK
    zv = zv.reshape(B, nc, SGU_CHUNK, N_SGU_GROUPS, SGU_GROUP_DIM)
    pos = jnp.arange(SGU_CHUNK, dtype=jnp.int32) // CHUNK
    mask = (pos[None, :] <= pos[:, None]).astype(w_s.dtype)
    w = w_s * mask[None]
    mixed = jnp.einsum('gij,bcjgd->bcigd', w, zv) + b_s.T[None, None, :, :, None]
    return zu * mixed.reshape(B, S, D_SGU)


def setup_inputs(seed: int = 0) -> dict:
    key = jax.random.key(seed)
    ks = jax.random.split(key, 14)
    f32 = jnp.float32
    nrm = lambda k, shape, s: jax.random.normal(k, shape, f32) * s
    head_bias = jnp.linspace(1.0, 5.0, N_ATTN_HEADS, dtype=f32)
    return {
        'x': jax.random.normal(ks[0], (BATCH, SEQ, D_MODEL), f32),
        'mix_norm_g': 1.0 + nrm(ks[1], (DEPTH, D_MODEL), 0.05),
        'w_in': nrm(ks[2], (DEPTH, D_MODEL, D_IN), D_MODEL ** -0.5),
        'b_f': head_bias[None, :] + nrm(ks[3], (DEPTH, N_ATTN_HEADS), 0.1),
        'sgu_ln_g': 1.0 + nrm(ks[4], (DEPTH, D_SGU), 0.05),
        'sgu_ln_b': nrm(ks[5], (DEPTH, D_SGU), 0.02),
        'w_s': nrm(ks[6], (DEPTH, N_SGU_GROUPS, SGU_CHUNK, SGU_CHUNK), 0.5 * SGU_CHUNK ** -0.5),
        'b_s': 1.0 + nrm(ks[7], (DEPTH, N_SGU_GROUPS, SGU_CHUNK), 0.1),
        'out_norm_g': 1.0 + nrm(ks[8], (DEPTH, D_MIX), 0.05),
        'w_out': nrm(ks[9], (DEPTH, D_MIX, D_MODEL), D_MIX ** -0.5),
        'ffn_norm_g': 1.0 + nrm(ks[10], (DEPTH, D_MODEL), 0.05),
        'w_gate_up': nrm(ks[11], (DEPTH, D_MODEL, 2 * D_FF), D_MODEL ** -0.5),
        'w_down': nrm(ks[12], (DEPTH, D_FF, D_MODEL), D_FF ** -0.5),
        'final_norm_g': 1.0 + nrm(ks[13], (D_MODEL,), 0.05),
    }


def reference(x, mix_norm_g, w_in, b_f, sgu_ln_g, sgu_ln_b, w_s, b_s, out_norm_g, w_out,
              ffn_norm_g, w_gate_up, w_down, final_norm_g):
    B, S, _ = x.shape
    for l in range(DEPTH):
        xn = rms_norm(x, mix_norm_g[l])
        h = xn @ w_in[l]
        q, k, v, f_logit, z = jnp.split(
            h, [D_ATTN, 2 * D_ATTN, 3 * D_ATTN, 3 * D_ATTN + N_ATTN_HEADS], axis=-1)
        q = q.reshape(B, S, N_ATTN_HEADS, HEAD_DIM)
        k = k.reshape(B, S, N_ATTN_HEADS, HEAD_DIM)
        v = v.reshape(B, S, N_ATTN_HEADS, HEAD_DIM)
        log_f = jax.nn.log_sigmoid(f_logit.astype(jnp.float32) + b_f[l].astype(jnp.float32))
        attn = forgetting_attention(q, k, v, log_f)
        sgu = spatial_gating(jax.nn.gelu(z, approximate=False),
                             sgu_ln_g[l], sgu_ln_b[l], w_s[l], b_s[l])
        merged = jnp.concatenate(
            [rms_norm(attn, out_norm_g[l, :D_ATTN]), rms_norm(sgu, out_norm_g[l, D_ATTN:])], axis=-1)
        x = x + merged @ w_out[l]
        xn = rms_norm(x, ffn_norm_g[l])
        gate, up = jnp.split(xn @ w_gate_up[l], 2, axis=-1)
        x = x + (jax.nn.silu(gate) * up) @ w_down[l]
    return rms_norm(x, final_norm_g)
```

```python
import functools
import math

import jax
import jax.numpy as jnp
from jax import lax
from jax.experimental import pallas as pl
from jax.experimental.pallas import tpu as pltpu

F32 = jnp.float32
BF16 = jnp.bfloat16

EPS = 1e-6
HEAD_DIM = 64
N_HEADS = 8
N_GROUPS = 8
GROUP_DIM = 64
SGU_CHUNK = 128
STREAM_CHUNK = 64
LANES = 128
F_PAD = LANES

PROJ_ROWS = 512
ATTN_TILE = 512
POST_ROWS = 512
FF_CHUNK = 256
VMEM_LIMIT = 56 * 1024 * 1024


def _rms_scale(x):
    return x * lax.rsqrt(jnp.mean(x * x, axis=-1, keepdims=True) + EPS)


def _resident(shape):
    zeros = (0,) * len(shape)
    return pl.BlockSpec(shape, lambda *_: zeros, pipeline_mode=pl.Buffered(1))


def _proj_kernel(x_ref, g_ref, wqkvf_ref, bf_ref, wz_ref, lng_ref, lnb_ref, ws_ref, bs_ref,
                 og_ref, q_ref, k_ref, v_ref, ct_ref, sgu_ref, carry_ref, mix_ref):
    rows = x_ref.shape[0]
    d_attn = q_ref.shape[1]
    d_sgu = sgu_ref.shape[1]

    @pl.when(pl.program_id(0) == 0)
    def _():
        carry_ref[...] = jnp.zeros_like(carry_ref)

    xn = _rms_scale(x_ref[...]) * g_ref[...]
    xb = xn.astype(BF16)

    h = jnp.dot(xb, wqkvf_ref[...], preferred_element_type=F32)
    q_ref[...] = (h[:, :d_attn] * (HEAD_DIM ** -0.5)).astype(BF16)
    k_ref[...] = h[:, d_attn:2 * d_attn].astype(BF16)
    v_ref[...] = h[:, 2 * d_attn:3 * d_attn].astype(BF16)

    f_t = jnp.transpose(h[:, 3 * d_attn:])[:N_HEADS, :] + bf_ref[...]
    log_f = jnp.minimum(f_t, 0.0) - jnp.log1p(jnp.exp(-jnp.abs(f_t)))
    lane = lax.broadcasted_iota(jnp.int32, log_f.shape, 1)
    c = log_f
    shift = 1
    while shift < rows:
        c = c + jnp.where(lane >= shift, pltpu.roll(c, shift, axis=1), 0.0)
        shift *= 2
    c = c + carry_ref[:, 0:1]
    ct_ref[...] = c
    carry_ref[...] = jnp.broadcast_to(c[:, rows - 1:rows], carry_ref.shape)

    z = jnp.dot(xb, wz_ref[...], preferred_element_type=F32)
    gz = 0.5 * z * (1.0 + lax.erf(z * (2.0 ** -0.5)))
    zu = gz[:, :d_sgu]
    zv = gz[:, d_sgu:]
    mu = jnp.mean(zv, axis=-1, keepdims=True)
    zc = zv - mu
    zvn = zc * lax.rsqrt(jnp.mean(zc * zc, axis=-1, keepdims=True) + EPS)
    zvb = (zvn * lng_ref[...] + lnb_ref[...]).astype(BF16)

    ri = lax.broadcasted_iota(jnp.int32, (SGU_CHUNK, SGU_CHUNK), 0) // STREAM_CHUNK
    ci = lax.broadcasted_iota(jnp.int32, (SGU_CHUNK, SGU_CHUNK), 1) // STREAM_CHUNK
    w_mask = (ci <= ri).astype(F32)
    wm = (ws_ref[...] * w_mask[None]).astype(BF16)
    low_half = lax.broadcasted_iota(jnp.int32, (SGU_CHUNK, LANES), 1) < GROUP_DIM
    for pair in range(N_GROUPS // 2):
        w_pair = wm[2 * pair:2 * pair + 2].reshape(2 * SGU_CHUNK, SGU_CHUNK)
        cols = slice(pair * LANES, (pair + 1) * LANES)
        for chunk in range(rows // SGU_CHUNK):
            rws = slice(chunk * SGU_CHUNK, (chunk + 1) * SGU_CHUNK)
            both = jnp.dot(w_pair, zvb[rws, cols], preferred_element_type=F32)
            mixed = jnp.where(low_half, both[:SGU_CHUNK], both[SGU_CHUNK:]) + bs_ref[pair]
            mix_ref[rws, cols] = zu[rws, cols] * mixed
    sgu_ref[...] = (_rms_scale(mix_ref[...]) * og_ref[...]).astype(BF16)


def _proj_call(x, g, wqkvf, bf, wz, lng, lnb, ws, bs_pair, og):
    s, d = x.shape
    d_attn = (wqkvf.shape[1] - F_PAD) // 3
    d_sgu = wz.shape[1] // 2
    rows = PROJ_ROWS
    row_blk = lambda w: pl.BlockSpec((rows, w), lambda i: (i, 0))
    return pl.pallas_call(
        _proj_kernel,
        grid=(s // rows,),
        in_specs=[row_blk(d), _resident(g.shape), _resident(wqkvf.shape), _resident(bf.shape),
                  _resident(wz.shape), _resident(lng.shape), _resident(lnb.shape),
                  _resident(ws.shape), _resident(bs_pair.shape), _resident(og.shape)],
        out_specs=[row_blk(d_attn), row_blk(d_attn), row_blk(d_attn),
                   pl.BlockSpec((N_HEADS, rows), lambda i: (0, i)), row_blk(d_sgu)],
        out_shape=[jax.ShapeDtypeStruct((s, d_attn), BF16)] * 3
        + [jax.ShapeDtypeStruct((N_HEADS, s), F32), jax.ShapeDtypeStruct((s, d_sgu), BF16)],
        scratch_shapes=[pltpu.VMEM((N_HEADS, LANES), F32), pltpu.VMEM((rows, d_sgu), F32)],
        compiler_params=pltpu.CompilerParams(
            dimension_semantics=("arbitrary",), vmem_limit_bytes=VMEM_LIMIT),
        name="fox_proj",
    )(x, g, wqkvf, bf, wz, lng, lnb, ws, bs_pair, og)


def _attn_kernel(q_ref, k_ref, v_ref, ccol_ref, crow_ref, o_ref, m_ref, l_ref, acc_ref):
    t = q_ref.shape[0]
    i = pl.program_id(1)
    q2 = q_ref[...]
    lane = lax.broadcasted_iota(jnp.int32, (t, LANES), 1)
    row = lax.broadcasted_iota(jnp.int32, (t, t), 0)
    col = lax.broadcasted_iota(jnp.int32, (t, t), 1)
    nt_dims = (((1,), (1,)), ((), ()))

    for hh in range(2):
        in_head = (lane >= HEAD_DIM) if hh else (lane < HEAD_DIM)
        qm = jnp.where(in_head, q2, jnp.zeros_like(q2))
        c_q = ccol_ref[0, :, hh:hh + 1]
        c0 = c_q[0:1, :]
        cq_rel = c_q - c0

        def scores(j):
            start = pl.multiple_of(j * t, t)
            k2 = k_ref[pl.ds(start, t), :]
            s = lax.dot_general(qm, k2, nt_dims, preferred_element_type=F32)
            c_k = crow_ref[0, hh:hh + 1, pl.ds(start, t)]
            return s + cq_rel + (c0 - c_k), start

        s, start = scores(i)
        s = jnp.where(col <= row, s, -jnp.inf)
        m = jnp.max(s, axis=-1, keepdims=True)
        p = jnp.exp(s - m)
        m_ref[hh] = m
        l_ref[hh] = jnp.sum(p, axis=-1, keepdims=True)
        acc_ref[hh] = jnp.dot(p.astype(BF16), v_ref[pl.ds(start, t), :],
                              preferred_element_type=F32)

        def body(step, carry):
            s, start = scores(i - 1 - step)
            m_old = m_ref[hh]
            m_new = jnp.maximum(m_old, jnp.max(s, axis=-1, keepdims=True))
            alpha = jnp.exp(m_old - m_new)
            p = jnp.exp(s - m_new)
            m_ref[hh] = m_new
            l_ref[hh] = alpha * l_ref[hh] + jnp.sum(p, axis=-1, keepdims=True)
            acc_ref[hh] = alpha * acc_ref[hh] + jnp.dot(
                p.astype(BF16), v_ref[pl.ds(start, t), :], preferred_element_type=F32)
            return carry

        lax.fori_loop(0, i, body, 0)

    out0 = acc_ref[0] / l_ref[0]
    out1 = acc_ref[1] / l_ref[1]
    o_ref[...] = jnp.where(lane < HEAD_DIM, out0, out1)


def _attn_call(q, k, v, ccol, crow):
    s, d_attn = q.shape
    t = ATTN_TILE
    pairs = d_attn // LANES
    return pl.pallas_call(
        _attn_kernel,
        grid=(pairs, s // t),
        in_specs=[pl.BlockSpec((t, LANES), lambda hp, i: (i, hp)),
                  pl.BlockSpec((s, LANES), lambda hp, i: (0, hp)),
                  pl.BlockSpec((s, LANES), lambda hp, i: (0, hp)),
                  pl.BlockSpec((1, t, 2), lambda hp, i: (hp, i, 0)),
                  pl.BlockSpec((1, 2, s), lambda hp, i: (hp, 0, 0))],
        out_specs=pl.BlockSpec((t, LANES), lambda hp, i: (i, hp)),
        out_shape=jax.ShapeDtypeStruct((s, d_attn), F32),
        scratch_shapes=[pltpu.VMEM((2, t, 1), F32), pltpu.VMEM((2, t, 1), F32),
                        pltpu.VMEM((2, t, LANES), F32)],
        compiler_params=pltpu.CompilerParams(
            dimension_semantics=("arbitrary", "arbitrary"), vmem_limit_bytes=VMEM_LIMIT),
        name="fox_attn",
    )(q, k, v, ccol, crow)


def _post_kernel(x_ref, attn_ref, sgu_ref, oga_ref, wout_ref, fg_ref, wgu_ref, wdown_ref,
                 fin_ref, o_ref, act_ref, *, final_norm):
    d_ff = wdown_ref.shape[0]
    an = (_rms_scale(attn_ref[...]) * oga_ref[...]).astype(BF16)
    merged = jnp.concatenate([an, sgu_ref[...]], axis=-1)
    x1 = x_ref[...] + jnp.dot(merged, wout_ref[...], preferred_element_type=F32)

    xb = (_rms_scale(x1) * fg_ref[...]).astype(BF16)
    for c in range(d_ff // FF_CHUNK):
        lo = c * FF_CHUNK
        gate = jnp.dot(xb, wgu_ref[:, lo:lo + FF_CHUNK], preferred_element_type=F32)
        up = jnp.dot(xb, wgu_ref[:, d_ff + lo:d_ff + lo + FF_CHUNK], preferred_element_type=F32)
        act_ref[:, lo:lo + FF_CHUNK] = (gate * jax.nn.sigmoid(gate) * up).astype(BF16)
    x2 = x1 + jnp.dot(act_ref[...], wdown_ref[...], preferred_element_type=F32)
    if final_norm:
        x2 = _rms_scale(x2) * fin_ref[...]
    o_ref[...] = x2


def _post_call(x, attn, sgu, oga, wout, fg, wgu, wdown, fin, final_norm):
    s, d = x.shape
    rows = POST_ROWS
    d_ff = wdown.shape[0]
    row_blk = lambda w: pl.BlockSpec((rows, w), lambda i: (i, 0))
    return pl.pallas_call(
        functools.partial(_post_kernel, final_norm=final_norm),
        grid=(s // rows,),
        in_specs=[row_blk(d), row_blk(attn.shape[1]), row_blk(sgu.shape[1]),
                  _resident(oga.shape), _resident(wout.shape), _resident(fg.shape),
                  _resident(wgu.shape), _resident(wdown.shape), _resident(fin.shape)],
        out_specs=row_blk(d),
        out_shape=jax.ShapeDtypeStruct((s, d), F32),
        scratch_shapes=[pltpu.VMEM((rows, d_ff), BF16)],
        compiler_params=pltpu.CompilerParams(
            dimension_semantics=("arbitrary",), vmem_limit_bytes=VMEM_LIMIT),
        name="fox_post",
    )(x, attn, sgu, oga, wout, fg, wgu, wdown, fin)


def kernel(x, mix_norm_g, w_in, b_f, sgu_ln_g, sgu_ln_b, w_s, b_s, out_norm_g, w_out,
           ffn_norm_g, w_gate_up, w_down, final_norm_g):
    batch, seq, d_model = x.shape
    depth = w_in.shape[0]
    d_attn = N_HEADS * HEAD_DIM
    d_sgu = N_GROUPS * GROUP_DIM
    assert batch == 1 and w_in.shape[2] == 3 * d_attn + N_HEADS + 2 * d_sgu
    assert seq % max(PROJ_ROWS, ATTN_TILE, POST_ROWS) == 0
    assert w_down.shape[1] % FF_CHUNK == 0 and w_s.shape[2] == SGU_CHUNK

    row = lambda a: a.reshape(1, -1).astype(F32)
    xs = x.reshape(seq, d_model).astype(F32)
    for l in range(depth):
        w = w_in[l]
        wqkvf = jnp.concatenate(
            [w[:, :3 * d_attn + N_HEADS],
             jnp.zeros((d_model, F_PAD - N_HEADS), w.dtype)], axis=1).astype(BF16)
        wz = w[:, 3 * d_attn + N_HEADS:].astype(BF16)
        bs_pair = jnp.repeat(
            b_s[l].reshape(N_GROUPS // 2, 2, SGU_CHUNK).transpose(0, 2, 1), GROUP_DIM, axis=2)
        q, k, v, c_t, sgu = _proj_call(
            xs, row(mix_norm_g[l]), wqkvf, b_f[l].reshape(N_HEADS, 1).astype(F32), wz,
            row(sgu_ln_g[l]), row(sgu_ln_b[l]), w_s[l].astype(F32), bs_pair.astype(F32),
            row(out_norm_g[l, d_attn:]))
        crow = c_t.reshape(N_HEADS // 2, 2, seq)
        ccol = crow.transpose(0, 2, 1)
        attn = _attn_call(q, k, v, ccol, crow)
        xs = _post_call(
            xs, attn, sgu, row(out_norm_g[l, :d_attn]), w_out[l].astype(BF16),
            row(ffn_norm_g[l]), w_gate_up[l].astype(BF16), w_down[l].astype(BF16),
            row(final_norm_g), final_norm=(l == depth - 1))
    return xs.reshape(batch, seq, d_model).astype(x.dtype)
```

```python
import functools
import math

import jax
import jax.numpy as jnp
from jax import lax
from jax.experimental import pallas as pl
from jax.experimental.pallas import tpu as pltpu

F32 = jnp.float32
BF16 = jnp.bfloat16

EPS = 1e-6
HEAD_DIM = 64
N_HEADS = 8
N_GROUPS = 8
GROUP_DIM = 64
SGU_CHUNK = 128
STREAM_CHUNK = 64
LANES = 128
F_PAD = LANES

PROJ_ROWS = 512
ATTN_TILE = 512
POST_ROWS = 512
FF_CHUNK = 256
LOG2E = math.log2(math.e)
QK_SCALE = HEAD_DIM ** -0.5 * LOG2E
NORM_SLACK = 1.01
PRUNE_BITS = 48.0
FEAT_STRIDE = 8
VMEM_LIMIT = 56 * 1024 * 1024


def _rms_scale(x):
    return x * lax.rsqrt(jnp.mean(x * x, axis=-1, keepdims=True) + EPS)


def _resident(shape):
    zeros = (0,) * len(shape)
    return pl.BlockSpec(shape, lambda *_: zeros, pipeline_mode=pl.Buffered(1))


def _split3(a):
    hi = a.astype(BF16).astype(F32)
    r = a - hi
    mid = r.astype(BF16).astype(F32)
    lo = (r - mid).astype(BF16).astype(F32)
    return hi, mid, lo


def _proj_kernel(x_ref, g_ref, wqkvf_ref, bf_ref, wz_ref, lng_ref, lnb_ref, ws_ref, bs_ref,
                 og_ref, q_ref, k_ref, v_ref, ct_ref, nrm_ref, sgu_ref, carry_ref, mix_ref):
    rows = x_ref.shape[0]
    d_attn = v_ref.shape[1]
    d_sgu = sgu_ref.shape[1]

    @pl.when(pl.program_id(0) == 0)
    def _():
        carry_ref[...] = jnp.zeros_like(carry_ref)

    xn = _rms_scale(x_ref[...]) * g_ref[...]
    xb = xn.astype(BF16)

    h = jnp.dot(xb, wqkvf_ref[...], preferred_element_type=F32)
    v_ref[...] = h[:, 2 * d_attn:3 * d_attn].astype(BF16)

    f_t = jnp.transpose(h[:, 3 * d_attn:])[:N_HEADS, :] + bf_ref[...]
    log_f = jnp.minimum(f_t, 0.0) - jnp.log1p(jnp.exp(-jnp.abs(f_t)))
    lane = lax.broadcasted_iota(jnp.int32, log_f.shape, 1)
    c = log_f
    shift = 1
    while shift < rows:
        c = c + jnp.where(lane >= shift, pltpu.roll(c, shift, axis=1), 0.0)
        shift *= 2
    c = c + carry_ref[:, 0:1]
    ct_ref[...] = c
    carry_ref[...] = jnp.broadcast_to(c[:, rows - 1:rows], carry_ref.shape)

    a = jnp.concatenate(
        [c[:, u:u + ATTN_TILE] - c[:, u:u + 1] for u in range(0, rows, ATTN_TILE)], axis=1)
    parts = _split3(a * LOG2E)
    sub = lax.broadcasted_iota(jnp.int32, (8, rows), 0)

    def feature_rows(head):
        pick = lambda part: jnp.broadcast_to(part[head:head + 1, :], (8, rows))
        return jnp.where(sub == 0, pick(parts[0]),
                         jnp.where(sub == 1, pick(parts[1]),
                                   jnp.where(sub == 2, pick(parts[2]), 0.0)))

    lane = lax.broadcasted_iota(jnp.int32, (rows, LANES), 1)
    lane1 = lax.broadcasted_iota(jnp.int32, (1, LANES), 1)
    pad = jnp.zeros((HEAD_DIM - 2 * FEAT_STRIDE, rows), F32)
    half_sel = (lax.broadcasted_iota(jnp.int32, (LANES, LANES), 0) // HEAD_DIM
                == lax.broadcasted_iota(jnp.int32, (LANES, LANES), 1)).astype(BF16)

    def max_row_norm(block_bf16):
        sq = jnp.square(block_bf16.astype(F32)).astype(BF16)
        nsq = jnp.dot(sq, half_sel, preferred_element_type=F32)
        return jnp.sqrt(jnp.max(nsq, axis=0, keepdims=True) * NORM_SLACK)

    for pair in range(N_HEADS // 2):
        f_even, f_odd = feature_rows(2 * pair), feature_rows(2 * pair + 1)
        feats = jnp.transpose(
            jnp.concatenate([f_odd, f_odd, pad, f_even, f_even, pad], axis=0))
        blk = slice(pair * LANES, (pair + 1) * LANES)
        qb = h[:, blk] * QK_SCALE
        kb = h[:, d_attn + pair * LANES:d_attn + (pair + 1) * LANES]
        nrm_ref[0, pair:pair + 1, :] = max_row_norm(qb.astype(BF16))
        nrm_ref[0, N_HEADS // 2 + pair:N_HEADS // 2 + pair + 1, :] = max_row_norm(kb.astype(BF16))
        for odd in (0, 1):
            base = 0 if odd else HEAD_DIM
            own = (lane >= HEAD_DIM) if odd else (lane < HEAD_DIM)
            first = lane < base + FEAT_STRIDE
            in_first = (lane1 >= base) & (lane1 < base + 3)
            in_second = (lane1 >= base + FEAT_STRIDE) & (lane1 < base + FEAT_STRIDE + 3)
            q_aug = jnp.where(own, qb, jnp.where(first, feats, jnp.where(in_second, -1.0, 0.0)))
            k_aug = jnp.where(own, kb, jnp.where(first, jnp.where(in_first, 1.0, 0.0), feats))
            dst = slice((2 * pair + odd) * LANES, (2 * pair + odd + 1) * LANES)
            q_ref[:, dst] = q_aug.astype(BF16)
            k_ref[:, dst] = k_aug.astype(BF16)

    z = jnp.dot(xb, wz_ref[...], preferred_element_type=F32)
    gz = 0.5 * z * (1.0 + lax.erf(z * (2.0 ** -0.5)))
    zu = gz[:, :d_sgu]
    zv = gz[:, d_sgu:]
    mu = jnp.mean(zv, axis=-1, keepdims=True)
    zc = zv - mu
    zvn = zc * lax.rsqrt(jnp.mean(zc * zc, axis=-1, keepdims=True) + EPS)
    zvb = (zvn * lng_ref[...] + lnb_ref[...]).astype(BF16)

    ri = lax.broadcasted_iota(jnp.int32, (SGU_CHUNK, SGU_CHUNK), 0) // STREAM_CHUNK
    ci = lax.broadcasted_iota(jnp.int32, (SGU_CHUNK, SGU_CHUNK), 1) // STREAM_CHUNK
    w_mask = (ci <= ri).astype(F32)
    wm = (ws_ref[...] * w_mask[None]).astype(BF16)
    low_half = lax.broadcasted_iota(jnp.int32, (SGU_CHUNK, LANES), 1) < GROUP_DIM
    for pair in range(N_GROUPS // 2):
        w_pair = wm[2 * pair:2 * pair + 2].reshape(2 * SGU_CHUNK, SGU_CHUNK)
        cols = slice(pair * LANES, (pair + 1) * LANES)
        for chunk in range(rows // SGU_CHUNK):
            rws = slice(chunk * SGU_CHUNK, (chunk + 1) * SGU_CHUNK)
            both = jnp.dot(w_pair, zvb[rws, cols], preferred_element_type=F32)
            mixed = jnp.where(low_half, both[:SGU_CHUNK], both[SGU_CHUNK:]) + bs_ref[pair]
            mix_ref[rws, cols] = zu[rws, cols] * mixed
    sgu_ref[...] = (_rms_scale(mix_ref[...]) * og_ref[...]).astype(BF16)


def _proj_call(x, g, wqkvf, bf, wz, lng, lnb, ws, bs_pair, og):
    s, d = x.shape
    d_attn = (wqkvf.shape[1] - F_PAD) // 3
    d_sgu = wz.shape[1] // 2
    rows = PROJ_ROWS
    row_blk = lambda w: pl.BlockSpec((rows, w), lambda i: (i, 0))
    return pl.pallas_call(
        _proj_kernel,
        grid=(s // rows,),
        in_specs=[row_blk(d), _resident(g.shape), _resident(wqkvf.shape), _resident(bf.shape),
                  _resident(wz.shape), _resident(lng.shape), _resident(lnb.shape),
                  _resident(ws.shape), _resident(bs_pair.shape), _resident(og.shape)],
        out_specs=[row_blk(N_HEADS * LANES), row_blk(N_HEADS * LANES), row_blk(d_attn),
                   pl.BlockSpec((N_HEADS, rows), lambda i: (0, i)),
                   pl.BlockSpec((1, N_HEADS, LANES), lambda i: (i, 0, 0)), row_blk(d_sgu)],
        out_shape=[jax.ShapeDtypeStruct((s, N_HEADS * LANES), BF16)] * 2
        + [jax.ShapeDtypeStruct((s, d_attn), BF16),
           jax.ShapeDtypeStruct((N_HEADS, s), F32),
           jax.ShapeDtypeStruct((s // rows, N_HEADS, LANES), F32),
           jax.ShapeDtypeStruct((s, d_sgu), BF16)],
        scratch_shapes=[pltpu.VMEM((N_HEADS, LANES), F32), pltpu.VMEM((rows, d_sgu), F32)],
        compiler_params=pltpu.CompilerParams(
            dimension_semantics=("arbitrary",), vmem_limit_bytes=VMEM_LIMIT),
        name="fox_proj",
    )(x, g, wqkvf, bf, wz, lng, lnb, ws, bs_pair, og)


def _attn_kernel(ctab_ref, qn_ref, kn_ref, q_ref, k_ref, v_ref, o_ref, m_ref, acc_ref):
    t = q_ref.shape[0]
    pair = pl.program_id(0)
    i = pl.program_id(1)

    def margin(head):
        kmax = lax.fori_loop(0, i + 1, lambda j, m: jnp.maximum(m, kn_ref[head, j]), 0.0)
        return 2.0 * qn_ref[head, i] * kmax + PRUNE_BITS

    margins = [margin(2 * pair + hh) for hh in range(2)]

    def needed(j):
        return functools.reduce(jnp.logical_or, [
            (ctab_ref[2 * pair + hh, i] - ctab_ref[2 * pair + hh, j + 1]) * LOG2E > -margins[hh]
            for hh in range(2)])

    j_lo = lax.while_loop(lambda j: jnp.logical_and(j > 0, needed(j - 1)), lambda j: j - 1, i)
    row = lax.broadcasted_iota(jnp.int32, (t, t), 0)
    col = lax.broadcasted_iota(jnp.int32, (t, t), 1)
    lane = lax.broadcasted_iota(jnp.int32, (t, LANES), 1)
    ones = jnp.ones((t, LANES), BF16)
    nt_dims = (((1,), (1,)), ((), ()))

    def scores(hh, start):
        hl = slice(hh * LANES, (hh + 1) * LANES)
        return lax.dot_general(q_ref[:, hl], k_ref[pl.ds(start, t), hl], nt_dims,
                               preferred_element_type=F32)

    def weighted_values(p, start):
        v_ext = jnp.concatenate([v_ref[pl.ds(start, t), :], ones], axis=1)
        return jnp.dot(p.astype(BF16), v_ext, preferred_element_type=F32)

    start = pl.multiple_of(i * t, t)
    for hh in range(2):
        s = jnp.where(col <= row, scores(hh, start), -jnp.inf)
        m = jnp.max(s, axis=-1, keepdims=True)
        m_ref[hh] = m
        acc_ref[hh] = weighted_values(jnp.exp2(s - m), start)

    def body(step, carry):
        j = i - 1 - step
        start = pl.multiple_of(j * t, t)
        for hh in range(2):
            head = 2 * pair + hh
            shift = (ctab_ref[head, i] - ctab_ref[head, j]) * LOG2E
            s = scores(hh, start)
            m_old = m_ref[hh]
            m_new = jnp.maximum(m_old, jnp.max(s, axis=-1, keepdims=True) + shift)
            p = jnp.exp2(s - (m_new - shift))
            m_ref[hh] = m_new
            acc_ref[hh] = jnp.exp2(m_old - m_new) * acc_ref[hh] + weighted_values(p, start)
        return carry

    lax.fori_loop(0, i - j_lo, body, 0)

    out0 = acc_ref[0, :, :LANES] / acc_ref[0, :, LANES:]
    out1 = acc_ref[1, :, :LANES] / acc_ref[1, :, LANES:]
    o_ref[...] = jnp.where(lane < HEAD_DIM, out0, out1)


def _attn_call(ctab, qn, kn, q, k, v):
    s, d_attn = v.shape
    t = ATTN_TILE
    pairs = d_attn // LANES
    return pl.pallas_call(
        _attn_kernel,
        grid=(pairs, s // t),
        in_specs=[pl.BlockSpec(memory_space=pltpu.SMEM)] * 3
                 + [pl.BlockSpec((t, 2 * LANES), lambda hp, i: (i, hp)),
                    pl.BlockSpec((s, 2 * LANES), lambda hp, i: (0, hp)),
                    pl.BlockSpec((s, LANES), lambda hp, i: (0, hp))],
        out_specs=pl.BlockSpec((t, LANES), lambda hp, i: (i, hp)),
        out_shape=jax.ShapeDtypeStruct((s, d_attn), F32),
        scratch_shapes=[pltpu.VMEM((2, t, 1), F32), pltpu.VMEM((2, t, 2 * LANES), F32)],
        compiler_params=pltpu.CompilerParams(
            dimension_semantics=("arbitrary", "arbitrary"), vmem_limit_bytes=VMEM_LIMIT),
        name="fox_attn",
    )(ctab, qn, kn, q, k, v)


def _post_kernel(x_ref, attn_ref, sgu_ref, oga_ref, wout_ref, fg_ref, wgu_ref, wdown_ref,
                 fin_ref, o_ref, act_ref, *, final_norm):
    d_ff = wdown_ref.shape[0]
    an = (_rms_scale(attn_ref[...]) * oga_ref[...]).astype(BF16)
    merged = jnp.concatenate([an, sgu_ref[...]], axis=-1)
    x1 = x_ref[...] + jnp.dot(merged, wout_ref[...], preferred_element_type=F32)

    xb = (_rms_scale(x1) * fg_ref[...]).astype(BF16)
    for c in range(d_ff // FF_CHUNK):
        lo = c * FF_CHUNK
        gate = jnp.dot(xb, wgu_ref[:, lo:lo + FF_CHUNK], preferred_element_type=F32)
        up = jnp.dot(xb, wgu_ref[:, d_ff + lo:d_ff + lo + FF_CHUNK], preferred_element_type=F32)
        act_ref[:, lo:lo + FF_CHUNK] = (gate * jax.nn.sigmoid(gate) * up).astype(BF16)
    x2 = x1 + jnp.dot(act_ref[...], wdown_ref[...], preferred_element_type=F32)
    if final_norm:
        x2 = _rms_scale(x2) * fin_ref[...]
    o_ref[...] = x2


def _post_call(x, attn, sgu, oga, wout, fg, wgu, wdown, fin, final_norm):
    s, d = x.shape
    rows = POST_ROWS
    d_ff = wdown.shape[0]
    row_blk = lambda w: pl.BlockSpec((rows, w), lambda i: (i, 0))
    return pl.pallas_call(
        functools.partial(_post_kernel, final_norm=final_norm),
        grid=(s // rows,),
        in_specs=[row_blk(d), row_blk(attn.shape[1]), row_blk(sgu.shape[1]),
                  _resident(oga.shape), _resident(wout.shape), _resident(fg.shape),
                  _resident(wgu.shape), _resident(wdown.shape), _resident(fin.shape)],
        out_specs=row_blk(d),
        out_shape=jax.ShapeDtypeStruct((s, d), F32),
        scratch_shapes=[pltpu.VMEM((rows, d_ff), BF16)],
        compiler_params=pltpu.CompilerParams(
            dimension_semantics=("arbitrary",), vmem_limit_bytes=VMEM_LIMIT),
        name="fox_post",
    )(x, attn, sgu, oga, wout, fg, wgu, wdown, fin)


def kernel(x, mix_norm_g, w_in, b_f, sgu_ln_g, sgu_ln_b, w_s, b_s, out_norm_g, w_out,
           ffn_norm_g, w_gate_up, w_down, final_norm_g):
    batch, seq, d_model = x.shape
    depth = w_in.shape[0]
    d_attn = N_HEADS * HEAD_DIM
    d_sgu = N_GROUPS * GROUP_DIM
    assert batch == 1 and w_in.shape[2] == 3 * d_attn + N_HEADS + 2 * d_sgu
    assert seq % max(PROJ_ROWS, POST_ROWS) == 0 and PROJ_ROWS == ATTN_TILE
    assert w_down.shape[1] % FF_CHUNK == 0 and w_s.shape[2] == SGU_CHUNK

    row = lambda a: a.reshape(1, -1).astype(F32)
    xs = x.reshape(seq, d_model).astype(F32)
    for l in range(depth):
        w = w_in[l]
        wqkvf = jnp.concatenate(
            [w[:, :3 * d_attn + N_HEADS],
             jnp.zeros((d_model, F_PAD - N_HEADS), w.dtype)], axis=1).astype(BF16)
        wz = w[:, 3 * d_attn + N_HEADS:].astype(BF16)
        bs_pair = jnp.repeat(
            b_s[l].reshape(N_GROUPS // 2, 2, SGU_CHUNK).transpose(0, 2, 1), GROUP_DIM, axis=2)
        q, k, v, c_t, norms, sgu = _proj_call(
            xs, row(mix_norm_g[l]), wqkvf, b_f[l].reshape(N_HEADS, 1).astype(F32), wz,
            row(sgu_ln_g[l]), row(sgu_ln_b[l]), w_s[l].astype(F32), bs_pair.astype(F32),
            row(out_norm_g[l, d_attn:]))
        tabs = norms[:, :, :2].reshape(-1, 2, N_HEADS // 2, 2).transpose(1, 2, 3, 0)
        tabs = tabs.reshape(2, N_HEADS, -1)
        attn = _attn_call(c_t[:, ::ATTN_TILE], tabs[0], tabs[1], q, k, v)
        xs = _post_call(
            xs, attn, sgu, row(out_norm_g[l, :d_attn]), w_out[l].astype(BF16),
            row(ffn_norm_g[l]), w_gate_up[l].astype(BF16), w_down[l].astype(BF16),
            row(final_norm_g), final_norm=(l == depth - 1))
    return xs.reshape(batch, seq, d_model).astype(x.dtype)
```

```python
import functools
import math

import jax
import jax.numpy as jnp
from jax import lax
from jax.experimental import pallas as pl
from jax.experimental.pallas import tpu as pltpu

F32 = jnp.float32
BF16 = jnp.bfloat16

EPS = 1e-6
HEAD_DIM = 64
N_HEADS = 8
N_GROUPS = 8
GROUP_DIM = 64
SGU_CHUNK = 128
STREAM_CHUNK = 64
LANES = 128
F_PAD = LANES

PROJ_ROWS = 512
ATTN_TILE = 512
POST_ROWS = 512
FF_CHUNK = 256
LOG2E = math.log2(math.e)
QK_SCALE = HEAD_DIM ** -0.5 * LOG2E
NORM_SLACK = 1.01
PRUNE_BITS = 48.0
ONES_ROWS = 16
FEAT_STRIDE = 8
VMEM_LIMIT = 56 * 1024 * 1024


def _rms_scale(x):
    return x * lax.rsqrt(jnp.mean(x * x, axis=-1, keepdims=True) + EPS)


def _resident(shape):
    zeros = (0,) * len(shape)
    return pl.BlockSpec(shape, lambda *_: zeros, pipeline_mode=pl.Buffered(1))


def _split3(a):
    hi = a.astype(BF16).astype(F32)
    r = a - hi
    mid = r.astype(BF16).astype(F32)
    lo = (r - mid).astype(BF16).astype(F32)
    return hi, mid, lo


def _proj_kernel(x_ref, g_ref, wqkvf_ref, bf_ref, wz_ref, lng_ref, lnb_ref, ws_ref, bs_ref,
                 og_ref, q_ref, k_ref, v_ref, ct_ref, nrm_ref, sgu_ref, carry_ref, mix_ref):
    rows = x_ref.shape[0]
    d_attn = v_ref.shape[0]
    d_sgu = sgu_ref.shape[1]

    @pl.when(pl.program_id(0) == 0)
    def _():
        carry_ref[...] = jnp.zeros_like(carry_ref)

    xn = _rms_scale(x_ref[...]) * g_ref[...]
    xb = xn.astype(BF16)

    h = jnp.dot(xb, wqkvf_ref[...], preferred_element_type=F32)
    for blk in range(d_attn // LANES):
        lo = 2 * d_attn + blk * LANES
        v_ref[blk * LANES:(blk + 1) * LANES, :] = jnp.transpose(h[:, lo:lo + LANES]).astype(BF16)

    f_t = jnp.transpose(h[:, 3 * d_attn:])[:N_HEADS, :] + bf_ref[...]
    log_f = jnp.minimum(f_t, 0.0) - jnp.log1p(jnp.exp(-jnp.abs(f_t)))
    lane = lax.broadcasted_iota(jnp.int32, log_f.shape, 1)
    c = log_f
    shift = 1
    while shift < rows:
        c = c + jnp.where(lane >= shift, pltpu.roll(c, shift, axis=1), 0.0)
        shift *= 2
    c = c + carry_ref[:, 0:1]
    ct_ref[...] = c
    carry_ref[...] = jnp.broadcast_to(c[:, rows - 1:rows], carry_ref.shape)

    a = jnp.concatenate(
        [c[:, u:u + ATTN_TILE] - c[:, u:u + 1] for u in range(0, rows, ATTN_TILE)], axis=1)
    parts = _split3(a * LOG2E)
    sub = lax.broadcasted_iota(jnp.int32, (8, rows), 0)

    def feature_rows(head):
        pick = lambda part: jnp.broadcast_to(part[head:head + 1, :], (8, rows))
        return jnp.where(sub == 0, pick(parts[0]),
                         jnp.where(sub == 1, pick(parts[1]),
                                   jnp.where(sub == 2, pick(parts[2]), 0.0)))

    lane = lax.broadcasted_iota(jnp.int32, (rows, LANES), 1)
    lane1 = lax.broadcasted_iota(jnp.int32, (1, LANES), 1)
    pad = jnp.zeros((HEAD_DIM - 2 * FEAT_STRIDE, rows), F32)
    half_sel = (lax.broadcasted_iota(jnp.int32, (LANES, LANES), 0) // HEAD_DIM
                == lax.broadcasted_iota(jnp.int32, (LANES, LANES), 1)).astype(BF16)

    def max_row_norm(block_bf16):
        sq = jnp.square(block_bf16.astype(F32)).astype(BF16)
        nsq = jnp.dot(sq, half_sel, preferred_element_type=F32)
        return jnp.sqrt(jnp.max(nsq, axis=0, keepdims=True) * NORM_SLACK)

    for pair in range(N_HEADS // 2):
        f_even, f_odd = feature_rows(2 * pair), feature_rows(2 * pair + 1)
        feats = jnp.transpose(
            jnp.concatenate([f_odd, f_odd, pad, f_even, f_even, pad], axis=0))
        blk = slice(pair * LANES, (pair + 1) * LANES)
        qb = h[:, blk] * QK_SCALE
        kb = h[:, d_attn + pair * LANES:d_attn + (pair + 1) * LANES]
        nrm_ref[0, pair:pair + 1, :] = max_row_norm(qb.astype(BF16))
        nrm_ref[0, N_HEADS // 2 + pair:N_HEADS // 2 + pair + 1, :] = max_row_norm(kb.astype(BF16))
        for odd in (0, 1):
            base = 0 if odd else HEAD_DIM
            own = (lane >= HEAD_DIM) if odd else (lane < HEAD_DIM)
            first = lane < base + FEAT_STRIDE
            in_first = (lane1 >= base) & (lane1 < base + 3)
            in_second = (lane1 >= base + FEAT_STRIDE) & (lane1 < base + FEAT_STRIDE + 3)
            q_aug = jnp.where(own, qb, jnp.where(first, feats, jnp.where(in_second, -1.0, 0.0)))
            k_aug = jnp.where(own, kb, jnp.where(first, jnp.where(in_first, 1.0, 0.0), feats))
            dst = slice((2 * pair + odd) * LANES, (2 * pair + odd + 1) * LANES)
            q_ref[:, dst] = q_aug.astype(BF16)
            k_ref[:, dst] = k_aug.astype(BF16)

    z = jnp.dot(xb, wz_ref[...], preferred_element_type=F32)
    gz = 0.5 * z * (1.0 + lax.erf(z * (2.0 ** -0.5)))
    zu = gz[:, :d_sgu]
    zv = gz[:, d_sgu:]
    mu = jnp.mean(zv, axis=-1, keepdims=True)
    zc = zv - mu
    zvn = zc * lax.rsqrt(jnp.mean(zc * zc, axis=-1, keepdims=True) + EPS)
    zvb = (zvn * lng_ref[...] + lnb_ref[...]).astype(BF16)

    ri = lax.broadcasted_iota(jnp.int32, (SGU_CHUNK, SGU_CHUNK), 0) // STREAM_CHUNK
    ci = lax.broadcasted_iota(jnp.int32, (SGU_CHUNK, SGU_CHUNK), 1) // STREAM_CHUNK
    w_mask = (ci <= ri).astype(F32)
    wm = (ws_ref[...] * w_mask[None]).astype(BF16)
    low_half = lax.broadcasted_iota(jnp.int32, (SGU_CHUNK, LANES), 1) < GROUP_DIM
    for pair in range(N_GROUPS // 2):
        w_pair = wm[2 * pair:2 * pair + 2].reshape(2 * SGU_CHUNK, SGU_CHUNK)
        cols = slice(pair * LANES, (pair + 1) * LANES)
        for chunk in range(rows // SGU_CHUNK):
            rws = slice(chunk * SGU_CHUNK, (chunk + 1) * SGU_CHUNK)
            both = jnp.dot(w_pair, zvb[rws, cols], preferred_element_type=F32)
            mixed = jnp.where(low_half, both[:SGU_CHUNK], both[SGU_CHUNK:]) + bs_ref[pair]
            mix_ref[rws, cols] = zu[rws, cols] * mixed
    sgu_ref[...] = (_rms_scale(mix_ref[...]) * og_ref[...]).astype(BF16)


def _proj_call(x, g, wqkvf, bf, wz, lng, lnb, ws, bs_pair, og):
    s, d = x.shape
    d_attn = (wqkvf.shape[1] - F_PAD) // 3
    d_sgu = wz.shape[1] // 2
    rows = PROJ_ROWS
    row_blk = lambda w: pl.BlockSpec((rows, w), lambda i: (i, 0))
    return pl.pallas_call(
        _proj_kernel,
        grid=(s // rows,),
        in_specs=[row_blk(d), _resident(g.shape), _resident(wqkvf.shape), _resident(bf.shape),
                  _resident(wz.shape), _resident(lng.shape), _resident(lnb.shape),
                  _resident(ws.shape), _resident(bs_pair.shape), _resident(og.shape)],
        out_specs=[row_blk(N_HEADS * LANES), row_blk(N_HEADS * LANES),
                   pl.BlockSpec((d_attn, rows), lambda i: (0, i)),
                   pl.BlockSpec((N_HEADS, rows), lambda i: (0, i)),
                   pl.BlockSpec((1, N_HEADS, LANES), lambda i: (i, 0, 0)), row_blk(d_sgu)],
        out_shape=[jax.ShapeDtypeStruct((s, N_HEADS * LANES), BF16)] * 2
        + [jax.ShapeDtypeStruct((d_attn, s), BF16),
           jax.ShapeDtypeStruct((N_HEADS, s), F32),
           jax.ShapeDtypeStruct((s // rows, N_HEADS, LANES), F32),
           jax.ShapeDtypeStruct((s, d_sgu), BF16)],
        scratch_shapes=[pltpu.VMEM((N_HEADS, LANES), F32), pltpu.VMEM((rows, d_sgu), F32)],
        compiler_params=pltpu.CompilerParams(
            dimension_semantics=("arbitrary",), vmem_limit_bytes=VMEM_LIMIT),
        name="fox_proj",
    )(x, g, wqkvf, bf, wz, lng, lnb, ws, bs_pair, og)


def _attn_kernel(ctab_ref, qn_ref, kn_ref, q_ref, k_ref, v_ref, o_ref, m_ref, acc_ref, s_ref):
    t = q_ref.shape[0]
    pair = pl.program_id(0)
    i = pl.program_id(1)

    def margin(head):
        kmax = lax.fori_loop(0, i + 1, lambda j, m: jnp.maximum(m, kn_ref[head, j]), 0.0)
        return 2.0 * qn_ref[head, i] * kmax + PRUNE_BITS

    margins = [margin(2 * pair + hh) for hh in range(2)]

    def needed(j):
        return functools.reduce(jnp.logical_or, [
            (ctab_ref[2 * pair + hh, i] - ctab_ref[2 * pair + hh, j + 1]) * LOG2E > -margins[hh]
            for hh in range(2)])

    j_lo = lax.while_loop(lambda j: jnp.logical_and(j > 0, needed(j - 1)), lambda j: j - 1, i)

    key = lax.broadcasted_iota(jnp.int32, (t, t), 0)
    qry = lax.broadcasted_iota(jnp.int32, (t, t), 1)
    ones = jnp.ones((ONES_ROWS, t), BF16)
    nt_dims = (((1,), (1,)), ((), ()))

    def scores_t(hh, start):
        hl = slice(hh * LANES, (hh + 1) * LANES)
        return lax.dot_general(k_ref[pl.ds(start, t), hl], q_ref[:, hl], nt_dims,
                               preferred_element_type=F32)

    def weighted_values_t(hh, p, start):
        v_t = v_ref[hh * HEAD_DIM:(hh + 1) * HEAD_DIM, pl.ds(start, t)]
        v_ext = jnp.concatenate([v_t, ones], axis=0)
        return jnp.dot(v_ext, p.astype(BF16), preferred_element_type=F32)

    start = pl.multiple_of(i * t, t)
    for hh in range(2):
        s = jnp.where(key <= qry, scores_t(hh, start), -jnp.inf)
        m = jnp.max(s, axis=0, keepdims=True)
        m_ref[hh] = m
        acc_ref[hh] = weighted_values_t(hh, jnp.exp2(s - m), start)

    def tile_start(j):
        return pl.multiple_of(jnp.maximum(j, 0) * t, t)

    def absorb(hh, j):
        head = 2 * pair + hh
        shift = (ctab_ref[head, i] - ctab_ref[head, j]) * LOG2E
        m_old = m_ref[hh]
        m_new = jnp.maximum(m_old, jnp.max(s_ref[hh], axis=0, keepdims=True) + shift)
        p = jnp.exp2(s_ref[hh] - (m_new - shift))
        m_ref[hh] = m_new
        acc_ref[hh] = (jnp.exp2(m_old - m_new) * acc_ref[hh]
                       + weighted_values_t(hh, p, tile_start(j)))

    n_off = i - j_lo

    @pl.when(n_off > 0)
    def _():
        s_ref[0] = scores_t(0, tile_start(i - 1))

    def body(step, carry):
        j = i - 1 - step
        s_ref[1] = scores_t(1, tile_start(j))
        absorb(0, j)
        s_ref[0] = scores_t(0, tile_start(j - 1))
        absorb(1, j)
        return carry

    lax.fori_loop(0, n_off, body, 0)

    out_t = jnp.concatenate(
        [acc_ref[hh, :HEAD_DIM, :] / acc_ref[hh, HEAD_DIM:HEAD_DIM + 1, :] for hh in range(2)],
        axis=0)
    o_ref[...] = jnp.transpose(out_t)


def _attn_call(ctab, qn, kn, q, k, v):
    d_attn, s = v.shape
    t = ATTN_TILE
    pairs = d_attn // LANES
    return pl.pallas_call(
        _attn_kernel,
        grid=(pairs, s // t),
        in_specs=[pl.BlockSpec(memory_space=pltpu.SMEM)] * 3
                 + [pl.BlockSpec((t, 2 * LANES), lambda hp, i: (i, hp)),
                    pl.BlockSpec((s, 2 * LANES), lambda hp, i: (0, hp)),
                    pl.BlockSpec((LANES, s), lambda hp, i: (hp, 0))],
        out_specs=pl.BlockSpec((t, LANES), lambda hp, i: (i, hp)),
        out_shape=jax.ShapeDtypeStruct((s, d_attn), F32),
        scratch_shapes=[pltpu.VMEM((2, 1, t), F32),
                        pltpu.VMEM((2, HEAD_DIM + ONES_ROWS, t), F32),
                        pltpu.VMEM((2, t, t), F32)],
        compiler_params=pltpu.CompilerParams(
            dimension_semantics=("arbitrary", "arbitrary"), vmem_limit_bytes=VMEM_LIMIT),
        name="fox_attn",
    )(ctab, qn, kn, q, k, v)


def _post_kernel(x_ref, attn_ref, sgu_ref, oga_ref, wout_ref, fg_ref, wgu_ref, wdown_ref,
                 fin_ref, o_ref, act_ref, *, final_norm):
    d_ff = wdown_ref.shape[0]
    an = (_rms_scale(attn_ref[...]) * oga_ref[...]).astype(BF16)
    merged = jnp.concatenate([an, sgu_ref[...]], axis=-1)
    x1 = x_ref[...] + jnp.dot(merged, wout_ref[...], preferred_element_type=F32)

    xb = (_rms_scale(x1) * fg_ref[...]).astype(BF16)
    for c in range(d_ff // FF_CHUNK):
        lo = c * FF_CHUNK
        gate = jnp.dot(xb, wgu_ref[:, lo:lo + FF_CHUNK], preferred_element_type=F32)
        up = jnp.dot(xb, wgu_ref[:, d_ff + lo:d_ff + lo + FF_CHUNK], preferred_element_type=F32)
        act_ref[:, lo:lo + FF_CHUNK] = (gate * jax.nn.sigmoid(gate) * up).astype(BF16)
    x2 = x1 + jnp.dot(act_ref[...], wdown_ref[...], preferred_element_type=F32)
    if final_norm:
        x2 = _rms_scale(x2) * fin_ref[...]
    o_ref[...] = x2


def _post_call(x, attn, sgu, oga, wout, fg, wgu, wdown, fin, final_norm):
    s, d = x.shape
    rows = POST_ROWS
    d_ff = wdown.shape[0]
    row_blk = lambda w: pl.BlockSpec((rows, w), lambda i: (i, 0))
    return pl.pallas_call(
        functools.partial(_post_kernel, final_norm=final_norm),
        grid=(s // rows,),
        in_specs=[row_blk(d), row_blk(attn.shape[1]), row_blk(sgu.shape[1]),
                  _resident(oga.shape), _resident(wout.shape), _resident(fg.shape),
                  _resident(wgu.shape), _resident(wdown.shape), _resident(fin.shape)],
        out_specs=row_blk(d),
        out_shape=jax.ShapeDtypeStruct((s, d), F32),
        scratch_shapes=[pltpu.VMEM((rows, d_ff), BF16)],
        compiler_params=pltpu.CompilerParams(
            dimension_semantics=("arbitrary",), vmem_limit_bytes=VMEM_LIMIT),
        name="fox_post",
    )(x, attn, sgu, oga, wout, fg, wgu, wdown, fin)


def kernel(x, mix_norm_g, w_in, b_f, sgu_ln_g, sgu_ln_b, w_s, b_s, out_norm_g, w_out,
           ffn_norm_g, w_gate_up, w_down, final_norm_g):
    batch, seq, d_model = x.shape
    depth = w_in.shape[0]
    d_attn = N_HEADS * HEAD_DIM
    d_sgu = N_GROUPS * GROUP_DIM
    assert batch == 1 and w_in.shape[2] == 3 * d_attn + N_HEADS + 2 * d_sgu
    assert seq % max(PROJ_ROWS, POST_ROWS) == 0 and PROJ_ROWS == ATTN_TILE
    assert w_down.shape[1] % FF_CHUNK == 0 and w_s.shape[2] == SGU_CHUNK

    row = lambda a: a.reshape(1, -1).astype(F32)
    xs = x.reshape(seq, d_model).astype(F32)
    for l in range(depth):
        w = w_in[l]
        wqkvf = jnp.concatenate(
            [w[:, :3 * d_attn + N_HEADS],
             jnp.zeros((d_model, F_PAD - N_HEADS), w.dtype)], axis=1).astype(BF16)
        wz = w[:, 3 * d_attn + N_HEADS:].astype(BF16)
        bs_pair = jnp.repeat(
            b_s[l].reshape(N_GROUPS // 2, 2, SGU_CHUNK).transpose(0, 2, 1), GROUP_DIM, axis=2)
        q, k, v, c_t, norms, sgu = _proj_call(
            xs, row(mix_norm_g[l]), wqkvf, b_f[l].reshape(N_HEADS, 1).astype(F32), wz,
            row(sgu_ln_g[l]), row(sgu_ln_b[l]), w_s[l].astype(F32), bs_pair.astype(F32),
            row(out_norm_g[l, d_attn:]))
        tabs = norms[:, :, :2].reshape(-1, 2, N_HEADS // 2, 2).transpose(1, 2, 3, 0)
        tabs = tabs.reshape(2, N_HEADS, -1)
        attn = _attn_call(c_t[:, ::ATTN_TILE], tabs[0], tabs[1], q, k, v)
        xs = _post_call(
            xs, attn, sgu, row(out_norm_g[l, :d_attn]), w_out[l].astype(BF16),
            row(ffn_norm_g[l]), w_gate_up[l].astype(BF16), w_down[l].astype(BF16),
            row(final_norm_g), final_norm=(l == depth - 1))
    return xs.reshape(batch, seq, d_model).astype(x.dtype)
```

```python
import functools
import math

import jax
import jax.numpy as jnp
from jax import lax
from jax.experimental import pallas as pl
from jax.experimental.pallas import tpu as pltpu

F32 = jnp.float32
BF16 = jnp.bfloat16

EPS = 1e-6
HEAD_DIM = 64
N_HEADS = 8
N_GROUPS = 8
GROUP_DIM = 64
SGU_CHUNK = 128
STREAM_CHUNK = 64
LANES = 128
F_PAD = LANES

PROJ_ROWS = 512
ATTN_TILE = 512
POST_ROWS = 512
FF_CHUNK = 256
LOG2E = math.log2(math.e)
QK_SCALE = HEAD_DIM ** -0.5 * LOG2E
NORM_SLACK = 1.01
PRUNE_BITS = 48.0
ONES_ROWS = 16
FEAT_STRIDE = 8
VMEM_LIMIT = 56 * 1024 * 1024


def _rms_scale(x):
    return x * lax.rsqrt(jnp.mean(x * x, axis=-1, keepdims=True) + EPS)


def _resident(shape):
    zeros = (0,) * len(shape)
    return pl.BlockSpec(shape, lambda *_: zeros, pipeline_mode=pl.Buffered(1))


def _split3(a):
    hi = a.astype(BF16).astype(F32)
    r = a - hi
    mid = r.astype(BF16).astype(F32)
    lo = (r - mid).astype(BF16).astype(F32)
    return hi, mid, lo


def _proj_kernel(x_ref, g_ref, wqkvf_ref, bf_ref, wz_ref, lng_ref, lnb_ref, ws_ref, bs_ref,
                 og_ref, q_ref, k_ref, v_ref, ct_ref, nrm_ref, sgu_ref, carry_ref, mix_ref):
    rows = x_ref.shape[0]
    d_attn = v_ref.shape[0]
    d_sgu = sgu_ref.shape[1]

    @pl.when(pl.program_id(0) == 0)
    def _():
        carry_ref[...] = jnp.zeros_like(carry_ref)

    xn = _rms_scale(x_ref[...]) * g_ref[...]
    xb = xn.astype(BF16)

    h = jnp.dot(xb, wqkvf_ref[...], preferred_element_type=F32)
    for blk in range(d_attn // LANES):
        lo = 2 * d_attn + blk * LANES
        v_ref[blk * LANES:(blk + 1) * LANES, :] = jnp.transpose(h[:, lo:lo + LANES]).astype(BF16)

    f_t = jnp.transpose(h[:, 3 * d_attn:])[:N_HEADS, :] + bf_ref[...]
    log_f = jnp.minimum(f_t, 0.0) - jnp.log1p(jnp.exp(-jnp.abs(f_t)))
    lane = lax.broadcasted_iota(jnp.int32, log_f.shape, 1)
    c = log_f
    shift = 1
    while shift < rows:
        c = c + jnp.where(lane >= shift, pltpu.roll(c, shift, axis=1), 0.0)
        shift *= 2
    c = c + carry_ref[:, 0:1]
    ct_ref[...] = c
    carry_ref[...] = jnp.broadcast_to(c[:, rows - 1:rows], carry_ref.shape)

    a = jnp.concatenate(
        [c[:, u:u + ATTN_TILE] - c[:, u:u + 1] for u in range(0, rows, ATTN_TILE)], axis=1)
    parts = _split3(a * LOG2E)
    sub = lax.broadcasted_iota(jnp.int32, (8, rows), 0)

    def feature_rows(head):
        pick = lambda part: jnp.broadcast_to(part[head:head + 1, :], (8, rows))
        return jnp.where(sub == 0, pick(parts[0]),
                         jnp.where(sub == 1, pick(parts[1]),
                                   jnp.where(sub == 2, pick(parts[2]), 0.0)))

    lane = lax.broadcasted_iota(jnp.int32, (rows, LANES), 1)
    lane1 = lax.broadcasted_iota(jnp.int32, (1, LANES), 1)
    pad = jnp.zeros((HEAD_DIM - 2 * FEAT_STRIDE, rows), F32)
    half_sel = (lax.broadcasted_iota(jnp.int32, (LANES, LANES), 0) // HEAD_DIM
                == lax.broadcasted_iota(jnp.int32, (LANES, LANES), 1)).astype(BF16)

    def max_row_norm(block_bf16):
        sq = jnp.square(block_bf16.astype(F32)).astype(BF16)
        nsq = jnp.dot(sq, half_sel, preferred_element_type=F32)
        return jnp.sqrt(jnp.max(nsq, axis=0, keepdims=True) * NORM_SLACK)

    for pair in range(N_HEADS // 2):
        f_even, f_odd = feature_rows(2 * pair), feature_rows(2 * pair + 1)
        feats = jnp.transpose(
            jnp.concatenate([f_odd, f_odd, pad, f_even, f_even, pad], axis=0))
        blk = slice(pair * LANES, (pair + 1) * LANES)
        qb = h[:, blk] * QK_SCALE
        kb = h[:, d_attn + pair * LANES:d_attn + (pair + 1) * LANES]
        nrm_ref[0, pair:pair + 1, :] = max_row_norm(qb.astype(BF16))
        nrm_ref[0, N_HEADS // 2 + pair:N_HEADS // 2 + pair + 1, :] = max_row_norm(kb.astype(BF16))
        for odd in (0, 1):
            base = 0 if odd else HEAD_DIM
            own = (lane >= HEAD_DIM) if odd else (lane < HEAD_DIM)
            first = lane < base + FEAT_STRIDE
            in_first = (lane1 >= base) & (lane1 < base + 3)
            in_second = (lane1 >= base + FEAT_STRIDE) & (lane1 < base + FEAT_STRIDE + 3)
            q_aug = jnp.where(own, qb, jnp.where(first, feats, jnp.where(in_second, -1.0, 0.0)))
            k_aug = jnp.where(own, kb, jnp.where(first, jnp.where(in_first, 1.0, 0.0), feats))
            dst = slice((2 * pair + odd) * LANES, (2 * pair + odd + 1) * LANES)
            q_ref[:, dst] = q_aug.astype(BF16)
            k_ref[:, dst] = k_aug.astype(BF16)

    z = jnp.dot(xb, wz_ref[...], preferred_element_type=F32)
    gz = 0.5 * z * (1.0 + lax.erf(z * (2.0 ** -0.5)))
    zu = gz[:, :d_sgu]
    zv = gz[:, d_sgu:]
    mu = jnp.mean(zv, axis=-1, keepdims=True)
    zc = zv - mu
    zvn = zc * lax.rsqrt(jnp.mean(zc * zc, axis=-1, keepdims=True) + EPS)
    zvb = (zvn * lng_ref[...] + lnb_ref[...]).astype(BF16)

    ri = lax.broadcasted_iota(jnp.int32, (SGU_CHUNK, SGU_CHUNK), 0) // STREAM_CHUNK
    ci = lax.broadcasted_iota(jnp.int32, (SGU_CHUNK, SGU_CHUNK), 1) // STREAM_CHUNK
    w_mask = (ci <= ri).astype(F32)
    wm = (ws_ref[...] * w_mask[None]).astype(BF16)
    low_half = lax.broadcasted_iota(jnp.int32, (SGU_CHUNK, LANES), 1) < GROUP_DIM
    for pair in range(N_GROUPS // 2):
        w_pair = wm[2 * pair:2 * pair + 2].reshape(2 * SGU_CHUNK, SGU_CHUNK)
        cols = slice(pair * LANES, (pair + 1) * LANES)
        for chunk in range(rows // SGU_CHUNK):
            rws = slice(chunk * SGU_CHUNK, (chunk + 1) * SGU_CHUNK)
            both = jnp.dot(w_pair, zvb[rws, cols], preferred_element_type=F32)
            mixed = jnp.where(low_half, both[:SGU_CHUNK], both[SGU_CHUNK:]) + bs_ref[pair]
            mix_ref[rws, cols] = zu[rws, cols] * mixed
    sgu_ref[...] = (_rms_scale(mix_ref[...]) * og_ref[...]).astype(BF16)


def _proj_call(x, g, wqkvf, bf, wz, lng, lnb, ws, bs_pair, og):
    s, d = x.shape
    d_attn = (wqkvf.shape[1] - F_PAD) // 3
    d_sgu = wz.shape[1] // 2
    rows = PROJ_ROWS
    row_blk = lambda w: pl.BlockSpec((rows, w), lambda i: (i, 0))
    return pl.pallas_call(
        _proj_kernel,
        grid=(s // rows,),
        in_specs=[row_blk(d), _resident(g.shape), _resident(wqkvf.shape), _resident(bf.shape),
                  _resident(wz.shape), _resident(lng.shape), _resident(lnb.shape),
                  _resident(ws.shape), _resident(bs_pair.shape), _resident(og.shape)],
        out_specs=[row_blk(N_HEADS * LANES), row_blk(N_HEADS * LANES),
                   pl.BlockSpec((d_attn, rows), lambda i: (0, i)),
                   pl.BlockSpec((N_HEADS, rows), lambda i: (0, i)),
                   pl.BlockSpec((1, N_HEADS, LANES), lambda i: (i, 0, 0)), row_blk(d_sgu)],
        out_shape=[jax.ShapeDtypeStruct((s, N_HEADS * LANES), BF16)] * 2
        + [jax.ShapeDtypeStruct((d_attn, s), BF16),
           jax.ShapeDtypeStruct((N_HEADS, s), F32),
           jax.ShapeDtypeStruct((s // rows, N_HEADS, LANES), F32),
           jax.ShapeDtypeStruct((s, d_sgu), BF16)],
        scratch_shapes=[pltpu.VMEM((N_HEADS, LANES), F32), pltpu.VMEM((rows, d_sgu), F32)],
        compiler_params=pltpu.CompilerParams(
            dimension_semantics=("arbitrary",), vmem_limit_bytes=VMEM_LIMIT),
        name="fox_proj",
    )(x, g, wqkvf, bf, wz, lng, lnb, ws, bs_pair, og)


def _attn_kernel(ctab_ref, qn_ref, kn_ref, q_ref, k_ref, v_ref, o_ref, m_ref, acc_ref, s_ref):
    t = q_ref.shape[0]
    pair = pl.program_id(0)
    i = pl.program_id(1)

    key = lax.broadcasted_iota(jnp.int32, (t, t), 0)
    qry = lax.broadcasted_iota(jnp.int32, (t, t), 1)
    ones = jnp.ones((ONES_ROWS, t), BF16)
    nt_dims = (((1,), (1,)), ((), ()))

    def scores_t(hh, start):
        hl = slice(hh * LANES, (hh + 1) * LANES)
        return lax.dot_general(k_ref[pl.ds(start, t), hl], q_ref[:, hl], nt_dims,
                               preferred_element_type=F32)

    def weighted_values_t(hh, p, start):
        v_t = v_ref[hh * HEAD_DIM:(hh + 1) * HEAD_DIM, pl.ds(start, t)]
        v_ext = jnp.concatenate([v_t, ones], axis=0)
        return jnp.dot(v_ext, p.astype(BF16), preferred_element_type=F32)

    def tile_start(j):
        return pl.multiple_of(jnp.maximum(j, 0) * t, t)

    def absorb(hh, j, diagonal):
        head = 2 * pair + hh
        shift = (ctab_ref[head, i] - ctab_ref[head, j]) * LOG2E
        if diagonal:
            s_ref[hh] = jnp.where(key <= qry, s_ref[hh], -jnp.inf)
        m_old = m_ref[hh]
        m_new = jnp.maximum(m_old, jnp.max(s_ref[hh], axis=0, keepdims=True) + shift)
        p = jnp.exp2(s_ref[hh] - (m_new - shift))
        m_ref[hh] = m_new
        acc_ref[hh] = (jnp.exp2(m_old - m_new) * acc_ref[hh]
                       + weighted_values_t(hh, p, tile_start(j)))

    def step(j, diagonal):
        s_ref[1] = scores_t(1, tile_start(j))
        absorb(0, j, diagonal)
        s_ref[0] = scores_t(0, tile_start(j - 1))
        absorb(1, j, diagonal)

    def largest_logit(head):
        kmax = lax.fori_loop(0, i + 1, lambda j, m: jnp.maximum(m, kn_ref[head, j]), 0.0)
        return qn_ref[head, i] * kmax

    logit_cap = [largest_logit(2 * pair + hh) for hh in range(2)]

    m_ref[...] = jnp.full(m_ref.shape, -jnp.inf, F32)
    acc_ref[...] = jnp.zeros(acc_ref.shape, F32)
    s_ref[0] = scores_t(0, tile_start(i))
    step(i, True)

    m_low = jnp.min(m_ref[...], axis=-1)
    margins = [logit_cap[hh] - m_low[hh, 0] + PRUNE_BITS for hh in range(2)]

    def needed(j):
        return functools.reduce(jnp.logical_or, [
            (ctab_ref[2 * pair + hh, i] - ctab_ref[2 * pair + hh, j + 1]) * LOG2E > -margins[hh]
            for hh in range(2)])

    j_lo = lax.while_loop(lambda j: jnp.logical_and(j > 0, needed(j - 1)), lambda j: j - 1, i)
    lax.fori_loop(0, i - j_lo, lambda u, carry: (step(i - 1 - u, False), carry)[1], 0)

    out_t = jnp.concatenate(
        [acc_ref[hh, :HEAD_DIM, :] / acc_ref[hh, HEAD_DIM:HEAD_DIM + 1, :] for hh in range(2)],
        axis=0)
    o_ref[...] = jnp.transpose(out_t)


def _attn_call(ctab, qn, kn, q, k, v):
    d_attn, s = v.shape
    t = ATTN_TILE
    pairs = d_attn // LANES
    return pl.pallas_call(
        _attn_kernel,
        grid=(pairs, s // t),
        in_specs=[pl.BlockSpec(memory_space=pltpu.SMEM)] * 3
                 + [pl.BlockSpec((t, 2 * LANES), lambda hp, i: (i, hp)),
                    pl.BlockSpec((s, 2 * LANES), lambda hp, i: (0, hp)),
                    pl.BlockSpec((LANES, s), lambda hp, i: (hp, 0))],
        out_specs=pl.BlockSpec((t, LANES), lambda hp, i: (i, hp)),
        out_shape=jax.ShapeDtypeStruct((s, d_attn), F32),
        scratch_shapes=[pltpu.VMEM((2, 1, t), F32),
                        pltpu.VMEM((2, HEAD_DIM + ONES_ROWS, t), F32),
                        pltpu.VMEM((2, t, t), F32)],
        compiler_params=pltpu.CompilerParams(
            dimension_semantics=("arbitrary", "arbitrary"), vmem_limit_bytes=VMEM_LIMIT),
        name="fox_attn",
    )(ctab, qn, kn, q, k, v)


def _post_kernel(x_ref, attn_ref, sgu_ref, oga_ref, wout_ref, fg_ref, wgu_ref, wdown_ref,
                 fin_ref, o_ref, act_ref, *, final_norm):
    d_ff = wdown_ref.shape[0]
    an = (_rms_scale(attn_ref[...]) * oga_ref[...]).astype(BF16)
    merged = jnp.concatenate([an, sgu_ref[...]], axis=-1)
    x1 = x_ref[...] + jnp.dot(merged, wout_ref[...], preferred_element_type=F32)

    xb = (_rms_scale(x1) * fg_ref[...]).astype(BF16)
    for c in range(d_ff // FF_CHUNK):
        lo = c * FF_CHUNK
        gate = jnp.dot(xb, wgu_ref[:, lo:lo + FF_CHUNK], preferred_element_type=F32)
        up = jnp.dot(xb, wgu_ref[:, d_ff + lo:d_ff + lo + FF_CHUNK], preferred_element_type=F32)
        act_ref[:, lo:lo + FF_CHUNK] = (gate * jax.nn.sigmoid(gate) * up).astype(BF16)
    x2 = x1 + jnp.dot(act_ref[...], wdown_ref[...], preferred_element_type=F32)
    if final_norm:
        x2 = _rms_scale(x2) * fin_ref[...]
    o_ref[...] = x2


def _post_call(x, attn, sgu, oga, wout, fg, wgu, wdown, fin, final_norm):
    s, d = x.shape
    rows = POST_ROWS
    d_ff = wdown.shape[0]
    row_blk = lambda w: pl.BlockSpec((rows, w), lambda i: (i, 0))
    return pl.pallas_call(
        functools.partial(_post_kernel, final_norm=final_norm),
        grid=(s // rows,),
        in_specs=[row_blk(d), row_blk(attn.shape[1]), row_blk(sgu.shape[1]),
                  _resident(oga.shape), _resident(wout.shape), _resident(fg.shape),
                  _resident(wgu.shape), _resident(wdown.shape), _resident(fin.shape)],
        out_specs=row_blk(d),
        out_shape=jax.ShapeDtypeStruct((s, d), F32),
        scratch_shapes=[pltpu.VMEM((rows, d_ff), BF16)],
        compiler_params=pltpu.CompilerParams(
            dimension_semantics=("arbitrary",), vmem_limit_bytes=VMEM_LIMIT),
        name="fox_post",
    )(x, attn, sgu, oga, wout, fg, wgu, wdown, fin)


def kernel(x, mix_norm_g, w_in, b_f, sgu_ln_g, sgu_ln_b, w_s, b_s, out_norm_g, w_out,
           ffn_norm_g, w_gate_up, w_down, final_norm_g):
    batch, seq, d_model = x.shape
    depth = w_in.shape[0]
    d_attn = N_HEADS * HEAD_DIM
    d_sgu = N_GROUPS * GROUP_DIM
    assert batch == 1 and w_in.shape[2] == 3 * d_attn + N_HEADS + 2 * d_sgu
    assert seq % max(PROJ_ROWS, POST_ROWS) == 0 and PROJ_ROWS == ATTN_TILE
    assert w_down.shape[1] % FF_CHUNK == 0 and w_s.shape[2] == SGU_CHUNK

    row = lambda a: a.reshape(1, -1).astype(F32)
    xs = x.reshape(seq, d_model).astype(F32)
    for l in range(depth):
        w = w_in[l]
        wqkvf = jnp.concatenate(
            [w[:, :3 * d_attn + N_HEADS],
             jnp.zeros((d_model, F_PAD - N_HEADS), w.dtype)], axis=1).astype(BF16)
        wz = w[:, 3 * d_attn + N_HEADS:].astype(BF16)
        bs_pair = jnp.repeat(
            b_s[l].reshape(N_GROUPS // 2, 2, SGU_CHUNK).transpose(0, 2, 1), GROUP_DIM, axis=2)
        q, k, v, c_t, norms, sgu = _proj_call(
            xs, row(mix_norm_g[l]), wqkvf, b_f[l].reshape(N_HEADS, 1).astype(F32), wz,
            row(sgu_ln_g[l]), row(sgu_ln_b[l]), w_s[l].astype(F32), bs_pair.astype(F32),
            row(out_norm_g[l, d_attn:]))
        tabs = norms[:, :, :2].reshape(-1, 2, N_HEADS // 2, 2).transpose(1, 2, 3, 0)
        tabs = tabs.reshape(2, N_HEADS, -1)
        attn = _attn_call(c_t[:, ::ATTN_TILE], tabs[0], tabs[1], q, k, v)
        xs = _post_call(
            xs, attn, sgu, row(out_norm_g[l, :d_attn]), w_out[l].astype(BF16),
            row(ffn_norm_g[l]), w_gate_up[l].astype(BF16), w_down[l].astype(BF16),
            row(final_norm_g), final_norm=(l == depth - 1))
    return xs.reshape(batch, seq, d_model).astype(x.dtype)
```

```python
import functools
import math

import jax
import jax.numpy as jnp
from jax import lax
from jax.experimental import pallas as pl
from jax.experimental.pallas import tpu as pltpu

F32 = jnp.float32
BF16 = jnp.bfloat16

EPS = 1e-6
HEAD_DIM = 64
N_HEADS = 8
N_GROUPS = 8
GROUP_DIM = 64
SGU_CHUNK = 128
STREAM_CHUNK = 64
LANES = 128
F_PAD = LANES

PROJ_ROWS = 512
ATTN_TILE = 512
POST_ROWS = 512
FF_CHUNK = 256
LOG2E = math.log2(math.e)
QK_SCALE = HEAD_DIM ** -0.5 * LOG2E
NORM_SLACK = 1.01
PRUNE_BITS = 40.0
ONES_ROWS = 16
FEAT_STRIDE = 8
VMEM_LIMIT = 56 * 1024 * 1024


def _rms_scale(x):
    return x * lax.rsqrt(jnp.mean(x * x, axis=-1, keepdims=True) + EPS)


def _resident(shape):
    zeros = (0,) * len(shape)
    return pl.BlockSpec(shape, lambda *_: zeros, pipeline_mode=pl.Buffered(1))


def _layer_resident(stacked, layer):
    tail = (0,) * (stacked.ndim - 1)
    return pl.BlockSpec((None,) + stacked.shape[1:], lambda *_: (layer,) + tail,
                        pipeline_mode=pl.Buffered(1))


def _split3(a):
    hi = a.astype(BF16).astype(F32)
    r = a - hi
    mid = r.astype(BF16).astype(F32)
    lo = (r - mid).astype(BF16).astype(F32)
    return hi, mid, lo


def _proj_kernel(x_ref, g_ref, wqkvf_ref, bf_ref, wz_ref, lng_ref, lnb_ref, ws_ref, bs_ref,
                 og_ref, q_ref, k_ref, v_ref, ct_ref, nrm_ref, sgu_ref, carry_ref, mix_ref):
    rows = x_ref.shape[0]
    d_attn = v_ref.shape[0]
    d_sgu = sgu_ref.shape[1]

    @pl.when(pl.program_id(0) == 0)
    def _():
        carry_ref[...] = jnp.zeros_like(carry_ref)

    xn = _rms_scale(x_ref[...]) * g_ref[...]
    xb = xn.astype(BF16)

    h = jnp.dot(xb, wqkvf_ref[...], preferred_element_type=F32)
    for blk in range(d_attn // LANES):
        lo = 2 * d_attn + blk * LANES
        v_ref[blk * LANES:(blk + 1) * LANES, :] = jnp.transpose(h[:, lo:lo + LANES]).astype(BF16)

    f_t = jnp.transpose(h[:, 3 * d_attn:])[:N_HEADS, :] + bf_ref[...]
    log_f = jnp.minimum(f_t, 0.0) - jnp.log1p(jnp.exp(-jnp.abs(f_t)))
    lane = lax.broadcasted_iota(jnp.int32, log_f.shape, 1)
    c = log_f
    shift = 1
    while shift < rows:
        c = c + jnp.where(lane >= shift, pltpu.roll(c, shift, axis=1), 0.0)
        shift *= 2
    c = c + carry_ref[:, 0:1]
    ct_ref[...] = c
    carry_ref[...] = jnp.broadcast_to(c[:, rows - 1:rows], carry_ref.shape)

    a = jnp.concatenate(
        [c[:, u:u + ATTN_TILE] - c[:, u:u + 1] for u in range(0, rows, ATTN_TILE)], axis=1)
    parts = _split3(a * LOG2E)
    sub = lax.broadcasted_iota(jnp.int32, (8, rows), 0)

    def feature_rows(head):
        pick = lambda part: jnp.broadcast_to(part[head:head + 1, :], (8, rows))
        return jnp.where(sub == 0, pick(parts[0]),
                         jnp.where(sub == 1, pick(parts[1]),
                                   jnp.where(sub == 2, pick(parts[2]), 0.0)))

    lane = lax.broadcasted_iota(jnp.int32, (rows, LANES), 1)
    lane1 = lax.broadcasted_iota(jnp.int32, (1, LANES), 1)
    pad = jnp.zeros((HEAD_DIM - 2 * FEAT_STRIDE, rows), F32)
    head_sel = (lax.broadcasted_iota(jnp.int32, (d_attn, LANES), 0) // HEAD_DIM
                == lax.broadcasted_iota(jnp.int32, (d_attn, LANES), 1)).astype(BF16)

    def max_row_norms(block):
        sq = jnp.square(block.astype(BF16).astype(F32)).astype(BF16)
        nsq = jnp.dot(sq, head_sel, preferred_element_type=F32)
        return jnp.sqrt(jnp.max(nsq, axis=0, keepdims=True) * NORM_SLACK)

    nrm_ref[0, 0:1, :] = max_row_norms(h[:, :d_attn] * QK_SCALE)
    nrm_ref[0, 1:2, :] = max_row_norms(h[:, d_attn:2 * d_attn])
    nrm_ref[0, 2:, :] = jnp.zeros((N_HEADS - 2, LANES), F32)

    for pair in range(N_HEADS // 2):
        f_even, f_odd = feature_rows(2 * pair), feature_rows(2 * pair + 1)
        feats = jnp.transpose(
            jnp.concatenate([f_odd, f_odd, pad, f_even, f_even, pad], axis=0))
        blk = slice(pair * LANES, (pair + 1) * LANES)
        qb = h[:, blk] * QK_SCALE
        kb = h[:, d_attn + pair * LANES:d_attn + (pair + 1) * LANES]
        for odd in (0, 1):
            base = 0 if odd else HEAD_DIM
            own = (lane >= HEAD_DIM) if odd else (lane < HEAD_DIM)
            first = lane < base + FEAT_STRIDE
            in_first = (lane1 >= base) & (lane1 < base + 3)
            in_second = (lane1 >= base + FEAT_STRIDE) & (lane1 < base + FEAT_STRIDE + 3)
            q_aug = jnp.where(own, qb, jnp.where(first, feats, jnp.where(in_second, -1.0, 0.0)))
            k_aug = jnp.where(own, kb, jnp.where(first, jnp.where(in_first, 1.0, 0.0), feats))
            dst = slice((2 * pair + odd) * LANES, (2 * pair + odd + 1) * LANES)
            q_ref[:, dst] = q_aug.astype(BF16)
            k_ref[:, dst] = k_aug.astype(BF16)

    z = jnp.dot(xb, wz_ref[...], preferred_element_type=F32)
    gz = 0.5 * z * (1.0 + lax.erf(z * (2.0 ** -0.5)))
    zu = gz[:, :d_sgu]
    zv = gz[:, d_sgu:]
    mu = jnp.mean(zv, axis=-1, keepdims=True)
    zc = zv - mu
    zvn = zc * lax.rsqrt(jnp.mean(zc * zc, axis=-1, keepdims=True) + EPS)
    zvb = (zvn * lng_ref[...] + lnb_ref[...]).astype(BF16)

    ri = lax.broadcasted_iota(jnp.int32, (SGU_CHUNK, SGU_CHUNK), 0) // STREAM_CHUNK
    ci = lax.broadcasted_iota(jnp.int32, (SGU_CHUNK, SGU_CHUNK), 1) // STREAM_CHUNK
    w_mask = (ci <= ri).astype(F32)
    wm = (ws_ref[...] * w_mask[None]).astype(BF16)
    low_half = lax.broadcasted_iota(jnp.int32, (SGU_CHUNK, LANES), 1) < GROUP_DIM
    for pair in range(N_GROUPS // 2):
        w_pair = wm[2 * pair:2 * pair + 2].reshape(2 * SGU_CHUNK, SGU_CHUNK)
        cols = slice(pair * LANES, (pair + 1) * LANES)
        for chunk in range(rows // SGU_CHUNK):
            rws = slice(chunk * SGU_CHUNK, (chunk + 1) * SGU_CHUNK)
            both = jnp.dot(w_pair, zvb[rws, cols], preferred_element_type=F32)
            mixed = jnp.where(low_half, both[:SGU_CHUNK], both[SGU_CHUNK:]) + bs_ref[pair]
            mix_ref[rws, cols] = zu[rws, cols] * mixed
    sgu_ref[...] = (_rms_scale(mix_ref[...]) * og_ref[...]).astype(BF16)


def _proj_call(layer, x, g, wqkvf, bf, wz, lng, lnb, ws, bs_pair, og):
    s, d = x.shape
    d_attn = (wqkvf.shape[2] - F_PAD) // 3
    d_sgu = wz.shape[2] // 2
    rows = PROJ_ROWS
    row_blk = lambda w: pl.BlockSpec((rows, w), lambda i: (i, 0))
    return pl.pallas_call(
        _proj_kernel,
        grid=(s // rows,),
        in_specs=[row_blk(d), _resident(g.shape), _layer_resident(wqkvf, layer),
                  _resident(bf.shape), _layer_resident(wz, layer),
                  _resident(lng.shape), _resident(lnb.shape),
                  _resident(ws.shape), _resident(bs_pair.shape), _resident(og.shape)],
        out_specs=[row_blk(N_HEADS * LANES), row_blk(N_HEADS * LANES),
                   pl.BlockSpec((d_attn, rows), lambda i: (0, i)),
                   pl.BlockSpec((N_HEADS, rows), lambda i: (0, i)),
                   pl.BlockSpec((1, N_HEADS, LANES), lambda i: (i, 0, 0)), row_blk(d_sgu)],
        out_shape=[jax.ShapeDtypeStruct((s, N_HEADS * LANES), BF16)] * 2
        + [jax.ShapeDtypeStruct((d_attn, s), BF16),
           jax.ShapeDtypeStruct((N_HEADS, s), F32),
           jax.ShapeDtypeStruct((s // rows, N_HEADS, LANES), F32),
           jax.ShapeDtypeStruct((s, d_sgu), BF16)],
        scratch_shapes=[pltpu.VMEM((N_HEADS, LANES), F32), pltpu.VMEM((rows, d_sgu), F32)],
        compiler_params=pltpu.CompilerParams(
            dimension_semantics=("arbitrary",), vmem_limit_bytes=VMEM_LIMIT),
        name="fox_proj",
    )(x, g, wqkvf, bf, wz, lng, lnb, ws, bs_pair, og)


def _attn_kernel(ctab_ref, qn_ref, kn_ref, q_ref, k_ref, v_ref, o_ref, m_ref, acc_ref, s_ref):
    t = q_ref.shape[0]
    pair = pl.program_id(0)
    i = pl.program_id(1)

    key = lax.broadcasted_iota(jnp.int32, (t, t), 0)
    qry = lax.broadcasted_iota(jnp.int32, (t, t), 1)
    ones = jnp.ones((ONES_ROWS, t), BF16)
    nt_dims = (((1,), (1,)), ((), ()))

    def scores_t(hh, start):
        hl = slice(hh * LANES, (hh + 1) * LANES)
        return lax.dot_general(k_ref[pl.ds(start, t), hl], q_ref[:, hl], nt_dims,
                               preferred_element_type=F32)

    def weighted_values_t(hh, p, start):
        v_t = v_ref[hh * HEAD_DIM:(hh + 1) * HEAD_DIM, pl.ds(start, t)]
        v_ext = jnp.concatenate([v_t, ones], axis=0)
        return jnp.dot(v_ext, p.astype(BF16), preferred_element_type=F32)

    def tile_start(j):
        return pl.multiple_of(jnp.maximum(j, 0) * t, t)

    def absorb(hh, j, diagonal):
        head = 2 * pair + hh
        shift = (ctab_ref[head, i] - ctab_ref[head, j]) * LOG2E
        if diagonal:
            s_ref[hh] = jnp.where(key <= qry, s_ref[hh], -jnp.inf)
        m_old = m_ref[hh]
        m_new = jnp.maximum(m_old, jnp.max(s_ref[hh], axis=0, keepdims=True) + shift)
        p = jnp.exp2(s_ref[hh] - (m_new - shift))
        m_ref[hh] = m_new
        acc_ref[hh] = (jnp.exp2(m_old - m_new) * acc_ref[hh]
                       + weighted_values_t(hh, p, tile_start(j)))

    def step(j, diagonal):
        s_ref[1] = scores_t(1, tile_start(j))
        absorb(0, j, diagonal)
        s_ref[0] = scores_t(0, tile_start(j - 1))
        absorb(1, j, diagonal)

    def largest_logit(head):
        kmax = lax.fori_loop(0, i + 1, lambda j, m: jnp.maximum(m, kn_ref[head, j]), 0.0)
        return qn_ref[head, i] * kmax

    logit_cap = [largest_logit(2 * pair + hh) for hh in range(2)]

    m_ref[...] = jnp.full(m_ref.shape, -jnp.inf, F32)
    acc_ref[...] = jnp.zeros(acc_ref.shape, F32)
    s_ref[0] = scores_t(0, tile_start(i))
    step(i, True)

    m_low = jnp.min(m_ref[...], axis=-1)
    margins = [logit_cap[hh] - m_low[hh, 0] + PRUNE_BITS for hh in range(2)]

    def needed(j):
        return functools.reduce(jnp.logical_or, [
            (ctab_ref[2 * pair + hh, i] - ctab_ref[2 * pair + hh, j + 1]) * LOG2E > -margins[hh]
            for hh in range(2)])

    j_lo = lax.while_loop(lambda j: jnp.logical_and(j > 0, needed(j - 1)), lambda j: j - 1, i)
    lax.fori_loop(0, i - j_lo, lambda u, carry: (step(i - 1 - u, False), carry)[1], 0)

    out_t = jnp.concatenate(
        [acc_ref[hh, :HEAD_DIM, :] / acc_ref[hh, HEAD_DIM:HEAD_DIM + 1, :] for hh in range(2)],
        axis=0)
    o_ref[...] = jnp.transpose(out_t)


def _attn_call(ctab, qn, kn, q, k, v):
    d_attn, s = v.shape
    t = ATTN_TILE
    pairs = d_attn // LANES
    return pl.pallas_call(
        _attn_kernel,
        grid=(pairs, s // t),
        in_specs=[pl.BlockSpec(memory_space=pltpu.SMEM)] * 3
                 + [pl.BlockSpec((t, 2 * LANES), lambda hp, i: (i, hp)),
                    pl.BlockSpec((s, 2 * LANES), lambda hp, i: (0, hp)),
                    pl.BlockSpec((LANES, s), lambda hp, i: (hp, 0))],
        out_specs=pl.BlockSpec((t, LANES), lambda hp, i: (i, hp)),
        out_shape=jax.ShapeDtypeStruct((s, d_attn), F32),
        scratch_shapes=[pltpu.VMEM((2, 1, t), F32),
                        pltpu.VMEM((2, HEAD_DIM + ONES_ROWS, t), F32),
                        pltpu.VMEM((2, t, t), F32)],
        compiler_params=pltpu.CompilerParams(
            dimension_semantics=("arbitrary", "arbitrary"), vmem_limit_bytes=VMEM_LIMIT),
        name="fox_attn",
    )(ctab, qn, kn, q, k, v)


def _post_kernel(x_ref, attn_ref, sgu_ref, oga_ref, wout_ref, fg_ref, wgu_ref, wdown_ref,
                 fin_ref, o_ref, act_ref, *, final_norm):
    d_ff = wdown_ref.shape[0]
    an = (_rms_scale(attn_ref[...]) * oga_ref[...]).astype(BF16)
    merged = jnp.concatenate([an, sgu_ref[...]], axis=-1)
    x1 = x_ref[...] + jnp.dot(merged, wout_ref[...], preferred_element_type=F32)

    xb = (_rms_scale(x1) * fg_ref[...]).astype(BF16)
    for c in range(d_ff // FF_CHUNK):
        lo = c * FF_CHUNK
        gate = jnp.dot(xb, wgu_ref[:, lo:lo + FF_CHUNK], preferred_element_type=F32)
        up = jnp.dot(xb, wgu_ref[:, d_ff + lo:d_ff + lo + FF_CHUNK], preferred_element_type=F32)
        act_ref[:, lo:lo + FF_CHUNK] = (gate * jax.nn.sigmoid(gate) * up).astype(BF16)
    x2 = x1 + jnp.dot(act_ref[...], wdown_ref[...], preferred_element_type=F32)
    if final_norm:
        x2 = _rms_scale(x2) * fin_ref[...]
    o_ref[...] = x2


def _post_call(layer, x, attn, sgu, oga, wout, fg, wgu, wdown, fin, final_norm):
    s, d = x.shape
    rows = POST_ROWS
    d_ff = wdown.shape[1]
    row_blk = lambda w: pl.BlockSpec((rows, w), lambda i: (i, 0))
    return pl.pallas_call(
        functools.partial(_post_kernel, final_norm=final_norm),
        grid=(s // rows,),
        in_specs=[row_blk(d), row_blk(attn.shape[1]), row_blk(sgu.shape[1]),
                  _resident(oga.shape), _layer_resident(wout, layer), _resident(fg.shape),
                  _layer_resident(wgu, layer), _layer_resident(wdown, layer),
                  _resident(fin.shape)],
        out_specs=row_blk(d),
        out_shape=jax.ShapeDtypeStruct((s, d), F32),
        scratch_shapes=[pltpu.VMEM((rows, d_ff), BF16)],
        compiler_params=pltpu.CompilerParams(
            dimension_semantics=("arbitrary",), vmem_limit_bytes=VMEM_LIMIT),
        name="fox_post",
    )(x, attn, sgu, oga, wout, fg, wgu, wdown, fin)


def kernel(x, mix_norm_g, w_in, b_f, sgu_ln_g, sgu_ln_b, w_s, b_s, out_norm_g, w_out,
           ffn_norm_g, w_gate_up, w_down, final_norm_g):
    batch, seq, d_model = x.shape
    depth = w_in.shape[0]
    d_attn = N_HEADS * HEAD_DIM
    d_sgu = N_GROUPS * GROUP_DIM
    assert batch == 1 and w_in.shape[2] == 3 * d_attn + N_HEADS + 2 * d_sgu
    assert seq % max(PROJ_ROWS, POST_ROWS) == 0 and PROJ_ROWS == ATTN_TILE
    assert w_down.shape[1] % FF_CHUNK == 0 and w_s.shape[2] == SGU_CHUNK

    row = lambda a: a.reshape(1, -1).astype(F32)
    xs = x.reshape(seq, d_model).astype(F32)
    n_qkvf = 3 * d_attn + N_HEADS
    wqkvf = jnp.pad(w_in[:, :, :n_qkvf], ((0, 0), (0, 0), (0, F_PAD - N_HEADS))).astype(BF16)
    wz = w_in[:, :, n_qkvf:].astype(BF16)
    wout, wgu, wdown = w_out.astype(BF16), w_gate_up.astype(BF16), w_down.astype(BF16)
    for l in range(depth):
        bs_pair = jnp.repeat(
            b_s[l].reshape(N_GROUPS // 2, 2, SGU_CHUNK).transpose(0, 2, 1), GROUP_DIM, axis=2)
        q, k, v, c_t, norms, sgu = _proj_call(
            l, xs, row(mix_norm_g[l]), wqkvf, b_f[l].reshape(N_HEADS, 1).astype(F32), wz,
            row(sgu_ln_g[l]), row(sgu_ln_b[l]), w_s[l].astype(F32), bs_pair.astype(F32),
            row(out_norm_g[l, d_attn:]))
        tabs = norms[:, :2, :N_HEADS].transpose(1, 2, 0)
        attn = _attn_call(c_t[:, ::ATTN_TILE], tabs[0], tabs[1], q, k, v)
        xs = _post_call(
            l, xs, attn, sgu, row(out_norm_g[l, :d_attn]), wout, row(ffn_norm_g[l]), wgu, wdown,
            row(final_norm_g), final_norm=(l == depth - 1))
    return xs.reshape(batch, seq, d_model).astype(x.dtype)
```

```python
import functools
import math

import jax
import jax.numpy as jnp
from jax import lax
from jax.experimental import pallas as pl
from jax.experimental.pallas import tpu as pltpu

F32 = jnp.float32
BF16 = jnp.bfloat16

EPS = 1e-6
HEAD_DIM = 64
N_HEADS = 8
N_GROUPS = 8
GROUP_DIM = 64
SGU_CHUNK = 128
STREAM_CHUNK = 64
LANES = 128
F_PAD = LANES

PROJ_ROWS = 512
ATTN_TILE = 512
POST_ROWS = 512
FF_CHUNK = 256
LOG2E = math.log2(math.e)
QK_SCALE = HEAD_DIM ** -0.5 * LOG2E
NORM_SLACK = 1.01
PRUNE_BITS = 40.0
ONES_ROWS = 16
FEAT_STRIDE = 8
VMEM_LIMIT = 56 * 1024 * 1024


def _rms_scale(x):
    return x * lax.rsqrt(jnp.mean(x * x, axis=-1, keepdims=True) + EPS)


def _resident(shape):
    zeros = (0,) * len(shape)
    return pl.BlockSpec(shape, lambda *_: zeros, pipeline_mode=pl.Buffered(1))


def _layer_resident(stacked, layer):
    tail = (0,) * (stacked.ndim - 1)
    return pl.BlockSpec((None,) + stacked.shape[1:], lambda *_: (layer,) + tail,
                        pipeline_mode=pl.Buffered(1))


def _split3(a):
    hi = a.astype(BF16).astype(F32)
    r = a - hi
    mid = r.astype(BF16).astype(F32)
    lo = (r - mid).astype(BF16).astype(F32)
    return hi, mid, lo


def _proj_kernel(x_ref, g_ref, wqkvf_ref, bf_ref, wz_ref, lng_ref, lnb_ref, ws_ref, bs_ref,
                 og_ref, q_ref, k_ref, v_ref, ct_ref, nrm_ref, sgu_ref, carry_ref, mix_ref,
                 h_ref, z_ref, zu_ref, zvb_ref):
    rows = x_ref.shape[0]
    d_attn = v_ref.shape[0]
    d_sgu = sgu_ref.shape[1]
    step = pl.program_id(0)

    @pl.when(step == 0)
    def _():
        h_ref[1] = jnp.zeros(h_ref.shape[1:], F32)
        z_ref[1] = jnp.zeros(z_ref.shape[1:], F32)

    @pl.when(step <= 1)
    def _():
        carry_ref[...] = jnp.zeros_like(carry_ref)

    def work(cur, prev):
        xb = (_rms_scale(x_ref[...]) * g_ref[...]).astype(BF16)
        z_ref[cur] = jnp.dot(xb, wz_ref[...], preferred_element_type=F32)

        z = z_ref[prev]
        gz = 0.5 * z * (1.0 + lax.erf(z * (2.0 ** -0.5)))
        zv = gz[:, d_sgu:]
        mu = jnp.mean(zv, axis=-1, keepdims=True)
        zc = zv - mu
        zvn = zc * lax.rsqrt(jnp.mean(zc * zc, axis=-1, keepdims=True) + EPS)
        zu_ref[...] = gz[:, :d_sgu]
        zvb_ref[...] = (zvn * lng_ref[...] + lnb_ref[...]).astype(BF16)

        h_ref[cur] = jnp.dot(xb, wqkvf_ref[...], preferred_element_type=F32)

        h = h_ref[prev]
        for blk in range(d_attn // LANES):
            lo = 2 * d_attn + blk * LANES
            v_ref[blk * LANES:(blk + 1) * LANES, :] = jnp.transpose(h[:, lo:lo + LANES]).astype(BF16)

        f_t = jnp.transpose(h[:, 3 * d_attn:])[:N_HEADS, :] + bf_ref[...]
        log_f = jnp.minimum(f_t, 0.0) - jnp.log1p(jnp.exp(-jnp.abs(f_t)))
        lane = lax.broadcasted_iota(jnp.int32, log_f.shape, 1)
        c = log_f
        shift = 1
        while shift < rows:
            c = c + jnp.where(lane >= shift, pltpu.roll(c, shift, axis=1), 0.0)
            shift *= 2
        c = c + carry_ref[:, 0:1]
        ct_ref[...] = c
        carry_ref[...] = jnp.broadcast_to(c[:, rows - 1:rows], carry_ref.shape)

        a = jnp.concatenate(
            [c[:, u:u + ATTN_TILE] - c[:, u:u + 1] for u in range(0, rows, ATTN_TILE)], axis=1)
        parts = _split3(a * LOG2E)
        sub = lax.broadcasted_iota(jnp.int32, (8, rows), 0)

        def feature_rows(head):
            pick = lambda part: jnp.broadcast_to(part[head:head + 1, :], (8, rows))
            return jnp.where(sub == 0, pick(parts[0]),
                             jnp.where(sub == 1, pick(parts[1]),
                                       jnp.where(sub == 2, pick(parts[2]), 0.0)))

        lane = lax.broadcasted_iota(jnp.int32, (rows, LANES), 1)
        lane1 = lax.broadcasted_iota(jnp.int32, (1, LANES), 1)
        pad = jnp.zeros((HEAD_DIM - 2 * FEAT_STRIDE, rows), F32)
        head_sel = (lax.broadcasted_iota(jnp.int32, (d_attn, LANES), 0) // HEAD_DIM
                    == lax.broadcasted_iota(jnp.int32, (d_attn, LANES), 1)).astype(BF16)

        def max_row_norms(block):
            b16 = block.astype(BF16)
            nsq = jnp.dot(b16 * b16, head_sel, preferred_element_type=F32)
            return jnp.sqrt(jnp.max(nsq, axis=0, keepdims=True) * NORM_SLACK)

        nrm_ref[0, 0:1, :] = max_row_norms(h[:, :d_attn] * QK_SCALE)
        nrm_ref[0, 1:2, :] = max_row_norms(h[:, d_attn:2 * d_attn])
        nrm_ref[0, 2:, :] = jnp.zeros((N_HEADS - 2, LANES), F32)

        for pair in range(N_HEADS // 2):
            f_even, f_odd = feature_rows(2 * pair), feature_rows(2 * pair + 1)
            feats = jnp.transpose(
                jnp.concatenate([f_odd, f_odd, pad, f_even, f_even, pad], axis=0))
            blk = slice(pair * LANES, (pair + 1) * LANES)
            qb = h[:, blk] * QK_SCALE
            kb = h[:, d_attn + pair * LANES:d_attn + (pair + 1) * LANES]
            for odd in (0, 1):
                base = 0 if odd else HEAD_DIM
                own = (lane >= HEAD_DIM) if odd else (lane < HEAD_DIM)
                first = lane < base + FEAT_STRIDE
                in_first = (lane1 >= base) & (lane1 < base + 3)
                in_second = (lane1 >= base + FEAT_STRIDE) & (lane1 < base + FEAT_STRIDE + 3)
                q_aug = jnp.where(own, qb,
                                  jnp.where(first, feats, jnp.where(in_second, -1.0, 0.0)))
                k_aug = jnp.where(own, kb,
                                  jnp.where(first, jnp.where(in_first, 1.0, 0.0), feats))
                dst = slice((2 * pair + odd) * LANES, (2 * pair + odd + 1) * LANES)
                q_ref[:, dst] = q_aug.astype(BF16)
                k_ref[:, dst] = k_aug.astype(BF16)

        ri = lax.broadcasted_iota(jnp.int32, (SGU_CHUNK, SGU_CHUNK), 0) // STREAM_CHUNK
        ci = lax.broadcasted_iota(jnp.int32, (SGU_CHUNK, SGU_CHUNK), 1) // STREAM_CHUNK
        w_mask = (ci <= ri).astype(F32)
        wm = (ws_ref[...] * w_mask[None]).astype(BF16)
        low_half = lax.broadcasted_iota(jnp.int32, (SGU_CHUNK, LANES), 1) < GROUP_DIM
        for pair in range(N_GROUPS // 2):
            w_pair = wm[2 * pair:2 * pair + 2].reshape(2 * SGU_CHUNK, SGU_CHUNK)
            cols = slice(pair * LANES, (pair + 1) * LANES)
            for chunk in range(rows // SGU_CHUNK):
                rws = slice(chunk * SGU_CHUNK, (chunk + 1) * SGU_CHUNK)
                both = jnp.dot(w_pair, zvb_ref[rws, cols], preferred_element_type=F32)
                mixed = jnp.where(low_half, both[:SGU_CHUNK], both[SGU_CHUNK:]) + bs_ref[pair]
                mix_ref[rws, cols] = zu_ref[rws, cols] * mixed
        sgu_ref[...] = (_rms_scale(mix_ref[...]) * og_ref[...]).astype(BF16)

    pl.when(step % 2 == 0)(lambda: work(0, 1))
    pl.when(step % 2 == 1)(lambda: work(1, 0))


def _proj_call(layer, x, g, wqkvf, bf, wz, lng, lnb, ws, bs_pair, og):
    s, d = x.shape
    d_attn = (wqkvf.shape[2] - F_PAD) // 3
    d_sgu = wz.shape[2] // 2
    rows = PROJ_ROWS
    tiles = s // rows
    done = lambda i: jnp.maximum(i - 1, 0)
    row_blk = lambda w: pl.BlockSpec((rows, w), lambda i: (done(i), 0))
    return pl.pallas_call(
        _proj_kernel,
        grid=(tiles + 1,),
        in_specs=[pl.BlockSpec((rows, d), lambda i: (jnp.minimum(i, tiles - 1), 0)),
                  _resident(g.shape), _layer_resident(wqkvf, layer),
                  _resident(bf.shape), _layer_resident(wz, layer),
                  _resident(lng.shape), _resident(lnb.shape),
                  _resident(ws.shape), _resident(bs_pair.shape), _resident(og.shape)],
        out_specs=[row_blk(N_HEADS * LANES), row_blk(N_HEADS * LANES),
                   pl.BlockSpec((d_attn, rows), lambda i: (0, done(i))),
                   pl.BlockSpec((N_HEADS, rows), lambda i: (0, done(i))),
                   pl.BlockSpec((1, N_HEADS, LANES), lambda i: (done(i), 0, 0)), row_blk(d_sgu)],
        out_shape=[jax.ShapeDtypeStruct((s, N_HEADS * LANES), BF16)] * 2
        + [jax.ShapeDtypeStruct((d_attn, s), BF16),
           jax.ShapeDtypeStruct((N_HEADS, s), F32),
           jax.ShapeDtypeStruct((tiles, N_HEADS, LANES), F32),
           jax.ShapeDtypeStruct((s, d_sgu), BF16)],
        scratch_shapes=[pltpu.VMEM((N_HEADS, LANES), F32), pltpu.VMEM((rows, d_sgu), F32),
                        pltpu.VMEM((2, rows, wqkvf.shape[2]), F32),
                        pltpu.VMEM((2, rows, 2 * d_sgu), F32),
                        pltpu.VMEM((rows, d_sgu), F32), pltpu.VMEM((rows, d_sgu), BF16)],
        compiler_params=pltpu.CompilerParams(
            dimension_semantics=("arbitrary",), vmem_limit_bytes=VMEM_LIMIT),
        name="fox_proj",
    )(x, g, wqkvf, bf, wz, lng, lnb, ws, bs_pair, og)


def _attn_kernel(ctab_ref, qn_ref, kn_ref, q_ref, k_ref, v_ref, o_ref, m_ref, acc_ref, s_ref):
    t = q_ref.shape[0]
    pair = pl.program_id(0)
    i = pl.program_id(1)

    key = lax.broadcasted_iota(jnp.int32, (t, t), 0)
    qry = lax.broadcasted_iota(jnp.int32, (t, t), 1)
    ones = jnp.ones((ONES_ROWS, t), BF16)
    nt_dims = (((1,), (1,)), ((), ()))

    def scores_t(hh, start):
        hl = slice(hh * LANES, (hh + 1) * LANES)
        return lax.dot_general(k_ref[pl.ds(start, t), hl], q_ref[:, hl], nt_dims,
                               preferred_element_type=F32)

    def weighted_values_t(hh, p, start):
        v_t = v_ref[hh * HEAD_DIM:(hh + 1) * HEAD_DIM, pl.ds(start, t)]
        v_ext = jnp.concatenate([v_t, ones], axis=0)
        return jnp.dot(v_ext, p.astype(BF16), preferred_element_type=F32)

    def tile_start(j):
        return pl.multiple_of(jnp.maximum(j, 0) * t, t)

    def absorb(hh, j, diagonal):
        head = 2 * pair + hh
        shift = (ctab_ref[head, i] - ctab_ref[head, j]) * LOG2E
        if diagonal:
            s_ref[hh] = jnp.where(key <= qry, s_ref[hh], -jnp.inf)
        m_old = m_ref[hh]
        m_new = jnp.maximum(m_old, jnp.max(s_ref[hh], axis=0, keepdims=True) + shift)
        p = jnp.exp2(s_ref[hh] - (m_new - shift))
        m_ref[hh] = m_new
        acc_ref[hh] = (jnp.exp2(m_old - m_new) * acc_ref[hh]
                       + weighted_values_t(hh, p, tile_start(j)))

    def step(j, diagonal):
        s_ref[1] = scores_t(1, tile_start(j))
        absorb(0, j, diagonal)
        s_ref[0] = scores_t(0, tile_start(j - 1))
        absorb(1, j, diagonal)

    def largest_logit(head):
        kmax = lax.fori_loop(0, i + 1, lambda j, m: jnp.maximum(m, kn_ref[head, j]), 0.0)
        return qn_ref[head, i] * kmax

    logit_cap = [largest_logit(2 * pair + hh) for hh in range(2)]

    m_ref[...] = jnp.full(m_ref.shape, -jnp.inf, F32)
    acc_ref[...] = jnp.zeros(acc_ref.shape, F32)
    s_ref[0] = scores_t(0, tile_start(i))
    step(i, True)

    m_low = jnp.min(m_ref[...], axis=-1)
    margins = [logit_cap[hh] - m_low[hh, 0] + PRUNE_BITS for hh in range(2)]

    def needed(j):
        return functools.reduce(jnp.logical_or, [
            (ctab_ref[2 * pair + hh, i] - ctab_ref[2 * pair + hh, j + 1]) * LOG2E > -margins[hh]
            for hh in range(2)])

    j_lo = lax.while_loop(lambda j: jnp.logical_and(j > 0, needed(j - 1)), lambda j: j - 1, i)
    lax.fori_loop(0, i - j_lo, lambda u, carry: (step(i - 1 - u, False), carry)[1], 0)

    out_t = jnp.concatenate(
        [acc_ref[hh, :HEAD_DIM, :] / acc_ref[hh, HEAD_DIM:HEAD_DIM + 1, :] for hh in range(2)],
        axis=0)
    o_ref[...] = jnp.transpose(out_t)


def _attn_call(ctab, qn, kn, q, k, v):
    d_attn, s = v.shape
    t = ATTN_TILE
    pairs = d_attn // LANES
    return pl.pallas_call(
        _attn_kernel,
        grid=(pairs, s // t),
        in_specs=[pl.BlockSpec(memory_space=pltpu.SMEM)] * 3
                 + [pl.BlockSpec((t, 2 * LANES), lambda hp, i: (i, hp)),
                    pl.BlockSpec((s, 2 * LANES), lambda hp, i: (0, hp)),
                    pl.BlockSpec((LANES, s), lambda hp, i: (hp, 0))],
        out_specs=pl.BlockSpec((t, LANES), lambda hp, i: (i, hp)),
        out_shape=jax.ShapeDtypeStruct((s, d_attn), F32),
        scratch_shapes=[pltpu.VMEM((2, 1, t), F32),
                        pltpu.VMEM((2, HEAD_DIM + ONES_ROWS, t), F32),
                        pltpu.VMEM((2, t, t), F32)],
        compiler_params=pltpu.CompilerParams(
            dimension_semantics=("arbitrary", "arbitrary"), vmem_limit_bytes=VMEM_LIMIT),
        name="fox_attn",
    )(ctab, qn, kn, q, k, v)


def _post_kernel(x_ref, attn_ref, sgu_ref, oga_ref, wout_ref, fg_ref, wgu_ref, wdown_ref,
                 fin_ref, o_ref, act_ref, *, final_norm):
    d_ff = wdown_ref.shape[0]
    an = (_rms_scale(attn_ref[...]) * oga_ref[...]).astype(BF16)
    merged = jnp.concatenate([an, sgu_ref[...]], axis=-1)
    x1 = x_ref[...] + jnp.dot(merged, wout_ref[...], preferred_element_type=F32)

    xb = (_rms_scale(x1) * fg_ref[...]).astype(BF16)
    for c in range(d_ff // FF_CHUNK):
        lo = c * FF_CHUNK
        gate = jnp.dot(xb, wgu_ref[:, lo:lo + FF_CHUNK], preferred_element_type=F32)
        up = jnp.dot(xb, wgu_ref[:, d_ff + lo:d_ff + lo + FF_CHUNK], preferred_element_type=F32)
        act_ref[:, lo:lo + FF_CHUNK] = (gate * jax.nn.sigmoid(gate) * up).astype(BF16)
    x2 = x1 + jnp.dot(act_ref[...], wdown_ref[...], preferred_element_type=F32)
    if final_norm:
        x2 = _rms_scale(x2) * fin_ref[...]
    o_ref[...] = x2


def _post_call(layer, x, attn, sgu, oga, wout, fg, wgu, wdown, fin, final_norm):
    s, d = x.shape
    rows = POST_ROWS
    d_ff = wdown.shape[1]
    row_blk = lambda w: pl.BlockSpec((rows, w), lambda i: (i, 0))
    return pl.pallas_call(
        functools.partial(_post_kernel, final_norm=final_norm),
        grid=(s // rows,),
        in_specs=[row_blk(d), row_blk(attn.shape[1]), row_blk(sgu.shape[1]),
                  _resident(oga.shape), _layer_resident(wout, layer), _resident(fg.shape),
                  _layer_resident(wgu, layer), _layer_resident(wdown, layer),
                  _resident(fin.shape)],
        out_specs=row_blk(d),
        out_shape=jax.ShapeDtypeStruct((s, d), F32),
        scratch_shapes=[pltpu.VMEM((rows, d_ff), BF16)],
        compiler_params=pltpu.CompilerParams(
            dimension_semantics=("arbitrary",), vmem_limit_bytes=VMEM_LIMIT),
        name="fox_post",
    )(x, attn, sgu, oga, wout, fg, wgu, wdown, fin)


def kernel(x, mix_norm_g, w_in, b_f, sgu_ln_g, sgu_ln_b, w_s, b_s, out_norm_g, w_out,
           ffn_norm_g, w_gate_up, w_down, final_norm_g):
    batch, seq, d_model = x.shape
    depth = w_in.shape[0]
    d_attn = N_HEADS * HEAD_DIM
    d_sgu = N_GROUPS * GROUP_DIM
    assert batch == 1 and w_in.shape[2] == 3 * d_attn + N_HEADS + 2 * d_sgu
    assert seq % max(PROJ_ROWS, POST_ROWS) == 0 and PROJ_ROWS == ATTN_TILE
    assert w_down.shape[1] % FF_CHUNK == 0 and w_s.shape[2] == SGU_CHUNK

    row = lambda a: a.reshape(1, -1).astype(F32)
    xs = x.reshape(seq, d_model).astype(F32)
    n_qkvf = 3 * d_attn + N_HEADS
    wqkvf = jnp.pad(w_in[:, :, :n_qkvf], ((0, 0), (0, 0), (0, F_PAD - N_HEADS))).astype(BF16)
    wz = w_in[:, :, n_qkvf:].astype(BF16)
    wout, wgu, wdown = w_out.astype(BF16), w_gate_up.astype(BF16), w_down.astype(BF16)
    for l in range(depth):
        bs_pair = jnp.repeat(
            b_s[l].reshape(N_GROUPS // 2, 2, SGU_CHUNK).transpose(0, 2, 1), GROUP_DIM, axis=2)
        q, k, v, c_t, norms, sgu = _proj_call(
            l, xs, row(mix_norm_g[l]), wqkvf, b_f[l].reshape(N_HEADS, 1).astype(F32), wz,
            row(sgu_ln_g[l]), row(sgu_ln_b[l]), w_s[l].astype(F32), bs_pair.astype(F32),
            row(out_norm_g[l, d_attn:]))
        tabs = norms[:, :2, :N_HEADS].transpose(1, 2, 0)
        attn = _attn_call(c_t[:, ::ATTN_TILE], tabs[0], tabs[1], q, k, v)
        xs = _post_call(
            l, xs, attn, sgu, row(out_norm_g[l, :d_attn]), wout, row(ffn_norm_g[l]), wgu, wdown,
            row(final_norm_g), final_norm=(l == depth - 1))
    return xs.reshape(batch, seq, d_model).astype(x.dtype)
```

```python
import functools
import math

import jax
import jax.numpy as jnp
from jax import lax
from jax.experimental import pallas as pl
from jax.experimental.pallas import tpu as pltpu

F32 = jnp.float32
BF16 = jnp.bfloat16

EPS = 1e-6
HEAD_DIM = 64
N_HEADS = 8
N_GROUPS = 8
GROUP_DIM = 64
SGU_CHUNK = 128
STREAM_CHUNK = 64
LANES = 128
F_PAD = LANES

PROJ_ROWS = 512
ATTN_TILE = 512
POST_ROWS = 512
FF_CHUNK = 256
LOG2E = math.log2(math.e)
QK_SCALE = HEAD_DIM ** -0.5 * LOG2E
NORM_SLACK = 1.01
PRUNE_BITS = 40.0
ONES_ROWS = 16
FEAT_STRIDE = 8
VMEM_LIMIT = 56 * 1024 * 1024


def _rms_scale(x):
    return x * lax.rsqrt(jnp.mean(x * x, axis=-1, keepdims=True) + EPS)


def _resident(shape):
    zeros = (0,) * len(shape)
    return pl.BlockSpec(shape, lambda *_: zeros, pipeline_mode=pl.Buffered(1))


def _layer_resident(stacked, layer):
    tail = (0,) * (stacked.ndim - 1)
    return pl.BlockSpec((None,) + stacked.shape[1:], lambda *_: (layer,) + tail,
                        pipeline_mode=pl.Buffered(1))


def _split3(a):
    hi = a.astype(BF16).astype(F32)
    r = a - hi
    mid = r.astype(BF16).astype(F32)
    lo = (r - mid).astype(BF16).astype(F32)
    return hi, mid, lo


def _proj_kernel(x_ref, g_ref, win_ref, bf_ref, lng_ref, lnb_ref, ws_ref, bs_ref,
                 og_ref, q_ref, k_ref, v_ref, ct_ref, nrm_ref, sgu_ref, carry_ref, mix_ref,
                 h_ref, z_ref, zu_ref, zvb_ref):
    rows = x_ref.shape[0]
    d_attn = v_ref.shape[0]
    d_sgu = sgu_ref.shape[1]
    step = pl.program_id(0)
    n_h = h_ref.shape[2]

    @pl.when(step == 0)
    def _():
        h_ref[1] = jnp.zeros(h_ref.shape[1:], F32)
        z_ref[1] = jnp.zeros(z_ref.shape[1:], F32)

    @pl.when(step <= 1)
    def _():
        carry_ref[...] = jnp.zeros_like(carry_ref)

    def work(cur, prev):
        xb = (_rms_scale(x_ref[...]) * g_ref[...]).astype(BF16)
        z_ref[cur] = jnp.dot(xb, win_ref[:, n_h:], preferred_element_type=F32)

        z = z_ref[prev]
        gz = 0.5 * z * (1.0 + lax.erf(z * (2.0 ** -0.5)))
        zv = gz[:, d_sgu:]
        mu = jnp.mean(zv, axis=-1, keepdims=True)
        zc = zv - mu
        zvn = zc * lax.rsqrt(jnp.mean(zc * zc, axis=-1, keepdims=True) + EPS)
        zu_ref[...] = gz[:, :d_sgu]
        zvb_ref[...] = (zvn * lng_ref[...] + lnb_ref[...]).astype(BF16)

        h_ref[cur] = jnp.dot(xb, win_ref[:, :n_h], preferred_element_type=F32)

        h = h_ref[prev]
        for blk in range(d_attn // LANES):
            lo = 2 * d_attn + blk * LANES
            v_ref[blk * LANES:(blk + 1) * LANES, :] = jnp.transpose(h[:, lo:lo + LANES]).astype(BF16)

        f_t = jnp.transpose(h[:, 3 * d_attn:])[:N_HEADS, :] + bf_ref[...]
        log_f = jnp.minimum(f_t, 0.0) - jnp.log1p(jnp.exp(-jnp.abs(f_t)))
        lane = lax.broadcasted_iota(jnp.int32, log_f.shape, 1)
        c = log_f
        shift = 1
        while shift < rows:
            c = c + jnp.where(lane >= shift, pltpu.roll(c, shift, axis=1), 0.0)
            shift *= 2
        c = c + carry_ref[:, 0:1]
        ct_ref[...] = c
        carry_ref[...] = jnp.broadcast_to(c[:, rows - 1:rows], carry_ref.shape)

        a = jnp.concatenate(
            [c[:, u:u + ATTN_TILE] - c[:, u:u + 1] for u in range(0, rows, ATTN_TILE)], axis=1)
        parts = _split3(a * LOG2E)
        sub = lax.broadcasted_iota(jnp.int32, (8, rows), 0)

        def feature_rows(head):
            pick = lambda part: jnp.broadcast_to(part[head:head + 1, :], (8, rows))
            return jnp.where(sub == 0, pick(parts[0]),
                             jnp.where(sub == 1, pick(parts[1]),
                                       jnp.where(sub == 2, pick(parts[2]), 0.0)))

        lane = lax.broadcasted_iota(jnp.int32, (rows, LANES), 1)
        lane1 = lax.broadcasted_iota(jnp.int32, (1, LANES), 1)
        pad = jnp.zeros((HEAD_DIM - 2 * FEAT_STRIDE, rows), F32)
        head_sel = (lax.broadcasted_iota(jnp.int32, (d_attn, LANES), 0) // HEAD_DIM
                    == lax.broadcasted_iota(jnp.int32, (d_attn, LANES), 1)).astype(BF16)

        def max_row_norms(block):
            b16 = block.astype(BF16)
            nsq = jnp.dot(b16 * b16, head_sel, preferred_element_type=F32)
            return jnp.sqrt(jnp.max(nsq, axis=0, keepdims=True) * NORM_SLACK)

        nrm_ref[0, 0:1, :] = max_row_norms(h[:, :d_attn] * QK_SCALE)
        nrm_ref[0, 1:2, :] = max_row_norms(h[:, d_attn:2 * d_attn])
        nrm_ref[0, 2:, :] = jnp.zeros((N_HEADS - 2, LANES), F32)

        for pair in range(N_HEADS // 2):
            f_even, f_odd = feature_rows(2 * pair), feature_rows(2 * pair + 1)
            feats = jnp.transpose(
                jnp.concatenate([f_odd, f_odd, pad, f_even, f_even, pad], axis=0))
            blk = slice(pair * LANES, (pair + 1) * LANES)
            qb = h[:, blk] * QK_SCALE
            kb = h[:, d_attn + pair * LANES:d_attn + (pair + 1) * LANES]
            for odd in (0, 1):
                base = 0 if odd else HEAD_DIM
                own = (lane >= HEAD_DIM) if odd else (lane < HEAD_DIM)
                first = lane < base + FEAT_STRIDE
                in_first = (lane1 >= base) & (lane1 < base + 3)
                in_second = (lane1 >= base + FEAT_STRIDE) & (lane1 < base + FEAT_STRIDE + 3)
                q_aug = jnp.where(own, qb,
                                  jnp.where(first, feats, jnp.where(in_second, -1.0, 0.0)))
                k_aug = jnp.where(own, kb,
                                  jnp.where(first, jnp.where(in_first, 1.0, 0.0), feats))
                dst = slice((2 * pair + odd) * LANES, (2 * pair + odd + 1) * LANES)
                q_ref[:, dst] = q_aug.astype(BF16)
                k_ref[:, dst] = k_aug.astype(BF16)

        ri = lax.broadcasted_iota(jnp.int32, (SGU_CHUNK, SGU_CHUNK), 0) // STREAM_CHUNK
        ci = lax.broadcasted_iota(jnp.int32, (SGU_CHUNK, SGU_CHUNK), 1) // STREAM_CHUNK
        w_mask = (ci <= ri).astype(F32)
        wm = (ws_ref[...] * w_mask[None]).astype(BF16)
        low_half = lax.broadcasted_iota(jnp.int32, (SGU_CHUNK, LANES), 1) < GROUP_DIM
        for pair in range(N_GROUPS // 2):
            w_pair = wm[2 * pair:2 * pair + 2].reshape(2 * SGU_CHUNK, SGU_CHUNK)
            cols = slice(pair * LANES, (pair + 1) * LANES)
            for chunk in range(rows // SGU_CHUNK):
                rws = slice(chunk * SGU_CHUNK, (chunk + 1) * SGU_CHUNK)
                both = jnp.dot(w_pair, zvb_ref[rws, cols], preferred_element_type=F32)
                mixed = jnp.where(low_half, both[:SGU_CHUNK], both[SGU_CHUNK:]) + bs_ref[pair]
                mix_ref[rws, cols] = zu_ref[rws, cols] * mixed
        sgu_ref[...] = (_rms_scale(mix_ref[...]) * og_ref[...]).astype(BF16)

    pl.when(step % 2 == 0)(lambda: work(0, 1))
    pl.when(step % 2 == 1)(lambda: work(1, 0))


def _proj_call(layer, x, g, win, bf, lng, lnb, ws, bs_pair, og):
    s, d = x.shape
    d_attn, d_sgu = N_HEADS * HEAD_DIM, N_GROUPS * GROUP_DIM
    n_h = 3 * d_attn + F_PAD
    rows = PROJ_ROWS
    tiles = s // rows
    done = lambda i: jnp.maximum(i - 1, 0)
    row_blk = lambda w: pl.BlockSpec((rows, w), lambda i: (done(i), 0))
    return pl.pallas_call(
        _proj_kernel,
        grid=(tiles + 1,),
        in_specs=[pl.BlockSpec((rows, d), lambda i: (jnp.minimum(i, tiles - 1), 0)),
                  _resident(g.shape), _layer_resident(win, layer), _resident(bf.shape),
                  _resident(lng.shape), _resident(lnb.shape),
                  _resident(ws.shape), _resident(bs_pair.shape), _resident(og.shape)],
        out_specs=[row_blk(N_HEADS * LANES), row_blk(N_HEADS * LANES),
                   pl.BlockSpec((d_attn, rows), lambda i: (0, done(i))),
                   pl.BlockSpec((N_HEADS, rows), lambda i: (0, done(i))),
                   pl.BlockSpec((1, N_HEADS, LANES), lambda i: (done(i), 0, 0)), row_blk(d_sgu)],
        out_shape=[jax.ShapeDtypeStruct((s, N_HEADS * LANES), BF16)] * 2
        + [jax.ShapeDtypeStruct((d_attn, s), BF16),
           jax.ShapeDtypeStruct((N_HEADS, s), F32),
           jax.ShapeDtypeStruct((tiles, N_HEADS, LANES), F32),
           jax.ShapeDtypeStruct((s, d_sgu), BF16)],
        scratch_shapes=[pltpu.VMEM((N_HEADS, LANES), F32), pltpu.VMEM((rows, d_sgu), F32),
                        pltpu.VMEM((2, rows, n_h), F32),
                        pltpu.VMEM((2, rows, 2 * d_sgu), F32),
                        pltpu.VMEM((rows, d_sgu), F32), pltpu.VMEM((rows, d_sgu), BF16)],
        compiler_params=pltpu.CompilerParams(
            dimension_semantics=("arbitrary",), vmem_limit_bytes=VMEM_LIMIT),
        name="fox_proj",
    )(x, g, win, bf, lng, lnb, ws, bs_pair, og)


def _attn_kernel(ctab_ref, qn_ref, kn_ref, q_ref, k_ref, v_ref, o_ref, m_ref, acc_ref, s_ref,
                 smax_ref):
    t = q_ref.shape[0]
    pair = pl.program_id(0)
    i = pl.program_id(1)

    key = lax.broadcasted_iota(jnp.int32, (t, t), 0)
    qry = lax.broadcasted_iota(jnp.int32, (t, t), 1)
    ones = jnp.ones((ONES_ROWS, t), BF16)
    nt_dims = (((1,), (1,)), ((), ()))

    def scores_t(hh, start):
        hl = slice(hh * LANES, (hh + 1) * LANES)
        return lax.dot_general(k_ref[pl.ds(start, t), hl], q_ref[:, hl], nt_dims,
                               preferred_element_type=F32)

    def weighted_values_t(hh, p, start):
        v_t = v_ref[hh * HEAD_DIM:(hh + 1) * HEAD_DIM, pl.ds(start, t)]
        v_ext = jnp.concatenate([v_t, ones], axis=0)
        return jnp.dot(v_ext, p.astype(BF16), preferred_element_type=F32)

    def tile_start(j):
        return pl.multiple_of(jnp.maximum(j, 0) * t, t)

    def put_scores(hh, s, diagonal):
        if diagonal:
            s = jnp.where(key <= qry, s, -jnp.inf)
        s_ref[hh] = s
        smax_ref[hh] = jnp.max(s, axis=0, keepdims=True)

    def weights(hh, j):
        head = 2 * pair + hh
        shift = (ctab_ref[head, i] - ctab_ref[head, j]) * LOG2E
        m_old = m_ref[hh]
        m_new = jnp.maximum(m_old, smax_ref[hh] + shift)
        m_ref[hh] = m_new
        return jnp.exp2(s_ref[hh] - (m_new - shift)), jnp.exp2(m_old - m_new)

    def accumulate(hh, j, p, alpha):
        acc_ref[hh] = alpha * acc_ref[hh] + weighted_values_t(hh, p, tile_start(j))

    def step(j, diagonal):
        put_scores(1, scores_t(1, tile_start(j)), diagonal)
        p0, alpha0 = weights(0, j)
        s_next = scores_t(0, tile_start(j - 1))
        accumulate(0, j, p0, alpha0)
        put_scores(0, s_next, False)
        p1, alpha1 = weights(1, j)
        accumulate(1, j, p1, alpha1)

    def largest_logit(head):
        kmax = lax.fori_loop(0, i + 1, lambda j, m: jnp.maximum(m, kn_ref[head, j]), 0.0)
        return qn_ref[head, i] * kmax

    logit_cap = [largest_logit(2 * pair + hh) for hh in range(2)]

    m_ref[...] = jnp.full(m_ref.shape, -jnp.inf, F32)
    acc_ref[...] = jnp.zeros(acc_ref.shape, F32)
    put_scores(0, scores_t(0, tile_start(i)), True)
    step(i, True)

    m_low = jnp.min(m_ref[...], axis=-1)
    margins = [logit_cap[hh] - m_low[hh, 0] + PRUNE_BITS for hh in range(2)]

    def needed(j):
        return functools.reduce(jnp.logical_or, [
            (ctab_ref[2 * pair + hh, i] - ctab_ref[2 * pair + hh, j + 1]) * LOG2E > -margins[hh]
            for hh in range(2)])

    j_lo = lax.while_loop(lambda j: jnp.logical_and(j > 0, needed(j - 1)), lambda j: j - 1, i)
    lax.fori_loop(0, i - j_lo, lambda u, carry: (step(i - 1 - u, False), carry)[1], 0)

    out_t = jnp.concatenate(
        [acc_ref[hh, :HEAD_DIM, :] / acc_ref[hh, HEAD_DIM:HEAD_DIM + 1, :] for hh in range(2)],
        axis=0)
    o_ref[...] = jnp.transpose(out_t)


def _attn_call(ctab, qn, kn, q, k, v):
    d_attn, s = v.shape
    t = ATTN_TILE
    pairs = d_attn // LANES
    return pl.pallas_call(
        _attn_kernel,
        grid=(pairs, s // t),
        in_specs=[pl.BlockSpec(memory_space=pltpu.SMEM)] * 3
                 + [pl.BlockSpec((t, 2 * LANES), lambda hp, i: (i, hp)),
                    pl.BlockSpec((s, 2 * LANES), lambda hp, i: (0, hp)),
                    pl.BlockSpec((LANES, s), lambda hp, i: (hp, 0))],
        out_specs=pl.BlockSpec((t, LANES), lambda hp, i: (i, hp)),
        out_shape=jax.ShapeDtypeStruct((s, d_attn), F32),
        scratch_shapes=[pltpu.VMEM((2, 1, t), F32),
                        pltpu.VMEM((2, HEAD_DIM + ONES_ROWS, t), F32),
                        pltpu.VMEM((2, t, t), F32), pltpu.VMEM((2, 1, t), F32)],
        compiler_params=pltpu.CompilerParams(
            dimension_semantics=("arbitrary", "arbitrary"), vmem_limit_bytes=VMEM_LIMIT),
        name="fox_attn",
    )(ctab, qn, kn, q, k, v)


def _post_kernel(x_ref, attn_ref, sgu_ref, oga_ref, wout_ref, fg_ref, wgu_ref, wdown_ref,
                 fin_ref, o_ref, act_ref, *, final_norm):
    d_ff = wdown_ref.shape[0]
    an = (_rms_scale(attn_ref[...]) * oga_ref[...]).astype(BF16)
    merged = jnp.concatenate([an, sgu_ref[...]], axis=-1)
    x1 = x_ref[...] + jnp.dot(merged, wout_ref[...], preferred_element_type=F32)

    xb = (_rms_scale(x1) * fg_ref[...]).astype(BF16)
    for c in range(d_ff // FF_CHUNK):
        lo = c * FF_CHUNK
        gate = jnp.dot(xb, wgu_ref[:, lo:lo + FF_CHUNK], preferred_element_type=F32)
        up = jnp.dot(xb, wgu_ref[:, d_ff + lo:d_ff + lo + FF_CHUNK], preferred_element_type=F32)
        act_ref[:, lo:lo + FF_CHUNK] = (gate * jax.nn.sigmoid(gate) * up).astype(BF16)
    x2 = x1 + jnp.dot(act_ref[...], wdown_ref[...], preferred_element_type=F32)
    if final_norm:
        x2 = _rms_scale(x2) * fin_ref[...]
    o_ref[...] = x2


def _post_call(layer, x, attn, sgu, oga, wout, fg, wgu, wdown, fin, final_norm):
    s, d = x.shape
    rows = POST_ROWS
    d_ff = wdown.shape[1]
    row_blk = lambda w: pl.BlockSpec((rows, w), lambda i: (i, 0))
    return pl.pallas_call(
        functools.partial(_post_kernel, final_norm=final_norm),
        grid=(s // rows,),
        in_specs=[row_blk(d), row_blk(attn.shape[1]), row_blk(sgu.shape[1]),
                  _resident(oga.shape), _layer_resident(wout, layer), _resident(fg.shape),
                  _layer_resident(wgu, layer), _layer_resident(wdown, layer),
                  _resident(fin.shape)],
        out_specs=row_blk(d),
        out_shape=jax.ShapeDtypeStruct((s, d), F32),
        scratch_shapes=[pltpu.VMEM((rows, d_ff), BF16)],
        compiler_params=pltpu.CompilerParams(
            dimension_semantics=("arbitrary",), vmem_limit_bytes=VMEM_LIMIT),
        name="fox_post",
    )(x, attn, sgu, oga, wout, fg, wgu, wdown, fin)


def kernel(x, mix_norm_g, w_in, b_f, sgu_ln_g, sgu_ln_b, w_s, b_s, out_norm_g, w_out,
           ffn_norm_g, w_gate_up, w_down, final_norm_g):
    batch, seq, d_model = x.shape
    depth = w_in.shape[0]
    d_attn = N_HEADS * HEAD_DIM
    d_sgu = N_GROUPS * GROUP_DIM
    assert batch == 1 and w_in.shape[2] == 3 * d_attn + N_HEADS + 2 * d_sgu
    assert seq % max(PROJ_ROWS, POST_ROWS) == 0 and PROJ_ROWS == ATTN_TILE
    assert w_down.shape[1] % FF_CHUNK == 0 and w_s.shape[2] == SGU_CHUNK

    row = lambda a: a.reshape(1, -1).astype(F32)
    xs = x.reshape(seq, d_model).astype(F32)
    n_qkvf = 3 * d_attn + N_HEADS
    win = jnp.concatenate(
        [w_in[:, :, :n_qkvf], jnp.zeros((depth, d_model, F_PAD - N_HEADS), w_in.dtype),
         w_in[:, :, n_qkvf:]], axis=2).astype(BF16)
    wout, wgu, wdown = w_out.astype(BF16), w_gate_up.astype(BF16), w_down.astype(BF16)
    for l in range(depth):
        bs_pair = jnp.repeat(
            b_s[l].reshape(N_GROUPS // 2, 2, SGU_CHUNK).transpose(0, 2, 1), GROUP_DIM, axis=2)
        q, k, v, c_t, norms, sgu = _proj_call(
            l, xs, row(mix_norm_g[l]), win, b_f[l].reshape(N_HEADS, 1).astype(F32),
            row(sgu_ln_g[l]), row(sgu_ln_b[l]), w_s[l].astype(F32), bs_pair.astype(F32),
            row(out_norm_g[l, d_attn:]))
        tabs = norms[:, :2, :N_HEADS].transpose(1, 2, 0)
        attn = _attn_call(c_t[:, ::ATTN_TILE], tabs[0], tabs[1], q, k, v)
        xs = _post_call(
            l, xs, attn, sgu, row(out_norm_g[l, :d_attn]), wout, row(ffn_norm_g[l]), wgu, wdown,
            row(final_norm_g), final_norm=(l == depth - 1))
    return xs.reshape(batch, seq, d_model).astype(x.dtype)
```

```python
import functools
import math

import jax
import jax.numpy as jnp
from jax import lax
from jax.experimental import pallas as pl
from jax.experimental.pallas import tpu as pltpu

F32 = jnp.float32
BF16 = jnp.bfloat16

EPS = 1e-6
HEAD_DIM = 64
N_HEADS = 8
N_GROUPS = 8
GROUP_DIM = 64
SGU_CHUNK = 128
STREAM_CHUNK = 64
LANES = 128
F_PAD = LANES

PROJ_ROWS = 512
ATTN_TILE = 512
POST_ROWS = 512
FF_CHUNK = 256
LOG2E = math.log2(math.e)
QK_SCALE = HEAD_DIM ** -0.5 * LOG2E
NORM_SLACK = 1.01
PRUNE_BITS = 40.0
ONES_ROWS = 16
FEAT_STRIDE = 8
VMEM_LIMIT = 56 * 1024 * 1024


def _rms_scale(x):
    return x * lax.rsqrt(jnp.mean(x * x, axis=-1, keepdims=True) + EPS)


def _resident(shape):
    zeros = (0,) * len(shape)
    return pl.BlockSpec(shape, lambda *_: zeros, pipeline_mode=pl.Buffered(1))


def _layer_resident(stacked, layer):
    tail = (0,) * (stacked.ndim - 1)
    return pl.BlockSpec((None,) + stacked.shape[1:], lambda *_: (layer,) + tail,
                        pipeline_mode=pl.Buffered(1))


def _split3(a):
    hi = a.astype(BF16).astype(F32)
    r = a - hi
    mid = r.astype(BF16).astype(F32)
    lo = (r - mid).astype(BF16).astype(F32)
    return hi, mid, lo


def _proj_kernel(x_ref, g_ref, win_ref, bf_ref, lng_ref, lnb_ref, ws_ref, bs_ref,
                 og_ref, q_ref, k_ref, v_ref, ct_ref, nrm_ref, sgu_ref, carry_ref, mix_ref,
                 h_ref, z_ref, zu_ref, zvb_ref):
    rows = x_ref.shape[0]
    d_attn = v_ref.shape[0]
    d_sgu = sgu_ref.shape[1]
    step = pl.program_id(0)
    n_h = h_ref.shape[2]

    @pl.when(step == 0)
    def _():
        h_ref[1] = jnp.zeros(h_ref.shape[1:], F32)
        z_ref[1] = jnp.zeros(z_ref.shape[1:], F32)

    @pl.when(step <= 1)
    def _():
        carry_ref[...] = jnp.zeros_like(carry_ref)

    def work(cur, prev):
        xb = (_rms_scale(x_ref[...]) * g_ref[...]).astype(BF16)
        z_ref[cur] = jnp.dot(xb, win_ref[:, n_h:], preferred_element_type=F32)

        z = z_ref[prev]
        gz = 0.5 * z * (1.0 + lax.erf(z * (2.0 ** -0.5)))
        zv = gz[:, d_sgu:]
        mu = jnp.mean(zv, axis=-1, keepdims=True)
        zc = zv - mu
        zvn = zc * lax.rsqrt(jnp.mean(zc * zc, axis=-1, keepdims=True) + EPS)
        zu_ref[...] = gz[:, :d_sgu]
        zvb_ref[...] = (zvn * lng_ref[...] + lnb_ref[...]).astype(BF16)

        h_ref[cur] = jnp.dot(xb, win_ref[:, :n_h], preferred_element_type=F32)

        h = h_ref[prev]
        for blk in range(d_attn // LANES):
            lo = 2 * d_attn + blk * LANES
            v_ref[blk * LANES:(blk + 1) * LANES, :] = jnp.transpose(h[:, lo:lo + LANES]).astype(BF16)

        f_t = jnp.transpose(h[:, 3 * d_attn:])[:N_HEADS, :] + bf_ref[...]
        log_f = jnp.minimum(f_t, 0.0) - jnp.log1p(jnp.exp(-jnp.abs(f_t)))
        lane = lax.broadcasted_iota(jnp.int32, log_f.shape, 1)
        c = log_f
        shift = 1
        while shift < rows:
            c = c + jnp.where(lane >= shift, pltpu.roll(c, shift, axis=1), 0.0)
            shift *= 2
        c = c + carry_ref[:, 0:1]
        ct_ref[...] = c
        carry_ref[...] = jnp.broadcast_to(c[:, rows - 1:rows], carry_ref.shape)

        a = jnp.concatenate(
            [c[:, u:u + ATTN_TILE] - c[:, u:u + 1] for u in range(0, rows, ATTN_TILE)], axis=1)
        parts = _split3(a * LOG2E)
        sub = lax.broadcasted_iota(jnp.int32, (8, rows), 0)

        def feature_rows(head):
            pick = lambda part: jnp.broadcast_to(part[head:head + 1, :], (8, rows))
            return jnp.where(sub == 0, pick(parts[0]),
                             jnp.where(sub == 1, pick(parts[1]),
                                       jnp.where(sub == 2, pick(parts[2]), 0.0)))

        lane = lax.broadcasted_iota(jnp.int32, (rows, LANES), 1)
        lane1 = lax.broadcasted_iota(jnp.int32, (1, LANES), 1)
        pad = jnp.zeros((HEAD_DIM - 2 * FEAT_STRIDE, rows), F32)
        head_sel = (lax.broadcasted_iota(jnp.int32, (d_attn, LANES), 0) // HEAD_DIM
                    == lax.broadcasted_iota(jnp.int32, (d_attn, LANES), 1)).astype(BF16)

        def max_row_norms(block):
            b16 = block.astype(BF16)
            nsq = jnp.dot(b16 * b16, head_sel, preferred_element_type=F32)
            return jnp.sqrt(jnp.max(nsq, axis=0, keepdims=True) * NORM_SLACK)

        nrm_ref[0, 0:1, :] = max_row_norms(h[:, :d_attn] * QK_SCALE)
        nrm_ref[0, 1:2, :] = max_row_norms(h[:, d_attn:2 * d_attn])
        nrm_ref[0, 2:, :] = jnp.zeros((N_HEADS - 2, LANES), F32)

        for pair in range(N_HEADS // 2):
            f_even, f_odd = feature_rows(2 * pair), feature_rows(2 * pair + 1)
            feats = jnp.transpose(
                jnp.concatenate([f_odd, f_odd, pad, f_even, f_even, pad], axis=0))
            blk = slice(pair * LANES, (pair + 1) * LANES)
            qb = h[:, blk] * QK_SCALE
            kb = h[:, d_attn + pair * LANES:d_attn + (pair + 1) * LANES]
            for odd in (0, 1):
                base = 0 if odd else HEAD_DIM
                own = (lane >= HEAD_DIM) if odd else (lane < HEAD_DIM)
                first = lane < base + FEAT_STRIDE
                in_first = (lane1 >= base) & (lane1 < base + 3)
                in_second = (lane1 >= base + FEAT_STRIDE) & (lane1 < base + FEAT_STRIDE + 3)
                q_aug = jnp.where(own, qb,
                                  jnp.where(first, feats, jnp.where(in_second, -1.0, 0.0)))
                k_aug = jnp.where(own, kb,
                                  jnp.where(first, jnp.where(in_first, 1.0, 0.0), feats))
                dst = slice((2 * pair + odd) * LANES, (2 * pair + odd + 1) * LANES)
                q_ref[:, dst] = q_aug.astype(BF16)
                k_ref[:, dst] = k_aug.astype(BF16)

        ri = lax.broadcasted_iota(jnp.int32, (SGU_CHUNK, SGU_CHUNK), 0) // STREAM_CHUNK
        ci = lax.broadcasted_iota(jnp.int32, (SGU_CHUNK, SGU_CHUNK), 1) // STREAM_CHUNK
        w_mask = (ci <= ri).astype(F32)
        wm = (ws_ref[...] * w_mask[None]).astype(BF16)
        low_half = lax.broadcasted_iota(jnp.int32, (SGU_CHUNK, LANES), 1) < GROUP_DIM
        for pair in range(N_GROUPS // 2):
            w_pair = wm[2 * pair:2 * pair + 2].reshape(2 * SGU_CHUNK, SGU_CHUNK)
            cols = slice(pair * LANES, (pair + 1) * LANES)
            for chunk in range(rows // SGU_CHUNK):
                rws = slice(chunk * SGU_CHUNK, (chunk + 1) * SGU_CHUNK)
                both = jnp.dot(w_pair, zvb_ref[rws, cols], preferred_element_type=F32)
                mixed = jnp.where(low_half, both[:SGU_CHUNK], both[SGU_CHUNK:]) + bs_ref[pair]
                mix_ref[rws, cols] = zu_ref[rws, cols] * mixed
        sgu_ref[...] = (_rms_scale(mix_ref[...]) * og_ref[...]).astype(BF16)

    pl.when(step % 2 == 0)(lambda: work(0, 1))
    pl.when(step % 2 == 1)(lambda: work(1, 0))


def _proj_call(layer, x, g, win, bf, lng, lnb, ws, bs_pair, og):
    s, d = x.shape
    d_attn, d_sgu = N_HEADS * HEAD_DIM, N_GROUPS * GROUP_DIM
    n_h = 3 * d_attn + F_PAD
    rows = PROJ_ROWS
    tiles = s // rows
    done = lambda i: jnp.maximum(i - 1, 0)
    row_blk = lambda w: pl.BlockSpec((rows, w), lambda i: (done(i), 0))
    return pl.pallas_call(
        _proj_kernel,
        grid=(tiles + 1,),
        in_specs=[pl.BlockSpec((rows, d), lambda i: (jnp.minimum(i, tiles - 1), 0)),
                  _resident(g.shape), _layer_resident(win, layer), _resident(bf.shape),
                  _resident(lng.shape), _resident(lnb.shape),
                  _resident(ws.shape), _resident(bs_pair.shape), _resident(og.shape)],
        out_specs=[row_blk(N_HEADS * LANES), row_blk(N_HEADS * LANES),
                   pl.BlockSpec((d_attn, rows), lambda i: (0, done(i))),
                   pl.BlockSpec((N_HEADS, rows), lambda i: (0, done(i))),
                   pl.BlockSpec((1, N_HEADS, LANES), lambda i: (done(i), 0, 0)), row_blk(d_sgu)],
        out_shape=[jax.ShapeDtypeStruct((s, N_HEADS * LANES), BF16)] * 2
        + [jax.ShapeDtypeStruct((d_attn, s), BF16),
           jax.ShapeDtypeStruct((N_HEADS, s), F32),
           jax.ShapeDtypeStruct((tiles, N_HEADS, LANES), F32),
           jax.ShapeDtypeStruct((s, d_sgu), BF16)],
        scratch_shapes=[pltpu.VMEM((N_HEADS, LANES), F32), pltpu.VMEM((rows, d_sgu), F32),
                        pltpu.VMEM((2, rows, n_h), F32),
                        pltpu.VMEM((2, rows, 2 * d_sgu), F32),
                        pltpu.VMEM((rows, d_sgu), F32), pltpu.VMEM((rows, d_sgu), BF16)],
        compiler_params=pltpu.CompilerParams(
            dimension_semantics=("arbitrary",), vmem_limit_bytes=VMEM_LIMIT),
        name="fox_proj",
    )(x, g, win, bf, lng, lnb, ws, bs_pair, og)


def _attn_kernel(ctab_ref, qn_ref, kn_ref, q_ref, k_ref, v_ref, o_ref, m_ref, acc_ref, s_ref,
                 smax_ref, pk_ref):
    t = q_ref.shape[0]
    pair = pl.program_id(0)
    i = pl.program_id(1)

    ones = jnp.ones((ONES_ROWS, t), BF16)
    nt_dims = (((1,), (1,)), ((), ()))

    def scores_t(hh, start):
        hl = slice(hh * LANES, (hh + 1) * LANES)
        return lax.dot_general(k_ref[pl.ds(start, t), hl], q_ref[:, hl], nt_dims,
                               preferred_element_type=F32)

    def weighted_values_t(hh, p, start):
        v_t = v_ref[hh * HEAD_DIM:(hh + 1) * HEAD_DIM, pl.ds(start, t)]
        v_ext = jnp.concatenate([v_t, ones], axis=0)
        return jnp.dot(v_ext, p.astype(BF16), preferred_element_type=F32)

    def tile_start(j):
        return pl.multiple_of(jnp.maximum(j, 0) * t, t)

    def put_scores(hh, s):
        s_ref[hh] = s
        smax_ref[hh] = jnp.max(s, axis=0, keepdims=True)

    def weights(hh, j):
        head = 2 * pair + hh
        shift = (ctab_ref[head, i] - ctab_ref[head, j]) * LOG2E
        m_old = m_ref[hh]
        m_new = jnp.maximum(m_old, smax_ref[hh] + shift)
        m_ref[hh] = m_new
        return jnp.exp2(s_ref[hh] - (m_new - shift)), jnp.exp2(m_old - m_new)

    def accumulate(hh, j, p, alpha):
        acc_ref[hh] = alpha * acc_ref[hh] + weighted_values_t(hh, p, tile_start(j))

    def step(j):
        put_scores(1, scores_t(1, tile_start(j)))
        p0, alpha0 = weights(0, j)
        s_next = scores_t(0, tile_start(j - 1))
        accumulate(0, j, p0, alpha0)
        put_scores(0, s_next)
        p1, alpha1 = weights(1, j)
        accumulate(1, j, p1, alpha1)

    half = t // 2
    start = tile_start(i)
    causal0 = (lax.broadcasted_iota(jnp.int32, (half, t), 0)
               <= lax.broadcasted_iota(jnp.int32, (half, t), 1))
    causal1 = (lax.broadcasted_iota(jnp.int32, (half, half), 0)
               <= lax.broadcasted_iota(jnp.int32, (half, half), 1))

    def put_diag_scores(hh):
        hl = slice(hh * LANES, (hh + 1) * LANES)
        s0 = lax.dot_general(k_ref[pl.ds(start, half), hl], q_ref[:, hl], nt_dims,
                             preferred_element_type=F32)
        s1 = lax.dot_general(k_ref[pl.ds(start + half, half), hl], q_ref[half:, hl], nt_dims,
                             preferred_element_type=F32)
        s0 = jnp.where(causal0, s0, -jnp.inf)
        s1 = jnp.where(causal1, s1, -jnp.inf)
        s_ref[hh, :half, :] = s0
        s_ref[hh, half:, half:] = s1
        m_ref[hh] = jnp.max(s0, axis=0, keepdims=True)
        m_ref[hh, :, half:] = jnp.maximum(m_ref[hh, :, half:], jnp.max(s1, axis=0, keepdims=True))

    def diag_weights(hh):
        return (jnp.exp2(s_ref[hh, :half, :] - m_ref[hh]).astype(BF16),
                jnp.exp2(s_ref[hh, half:, half:] - m_ref[hh, :, half:]).astype(BF16))

    def diag_accumulate(hh, p0, p1):
        rows_v = slice(hh * HEAD_DIM, (hh + 1) * HEAD_DIM)
        ones_h = jnp.ones((ONES_ROWS, half), BF16)
        v0 = jnp.concatenate([v_ref[rows_v, pl.ds(start, half)], ones_h], axis=0)
        v1 = jnp.concatenate(
            [v_ref[rows_v, pl.ds(pl.multiple_of(start + half, half), half)], ones_h], axis=0)
        acc_ref[hh] = jnp.dot(v0, p0, preferred_element_type=F32)
        acc_ref[hh, :, half:] += jnp.dot(v1, p1, preferred_element_type=F32)

    for hh in range(2):
        def scan(j, running, hh=hh):
            running = jnp.maximum(running, kn_ref[2 * pair + hh, j])
            pk_ref[hh, j] = running
            return running
        lax.fori_loop(0, i + 1, scan, 0.0)

    put_diag_scores(0)
    put_diag_scores(1)
    p00, p01 = diag_weights(0)
    s_next = scores_t(0, tile_start(i - 1))
    diag_accumulate(0, p00, p01)
    put_scores(0, s_next)
    p10, p11 = diag_weights(1)
    diag_accumulate(1, p10, p11)

    m_low = jnp.min(m_ref[...], axis=-1)
    floors = [m_low[hh, 0] - PRUNE_BITS for hh in range(2)]

    def needed(j):
        return functools.reduce(jnp.logical_or, [
            qn_ref[2 * pair + hh, i] * pk_ref[hh, j]
            + (ctab_ref[2 * pair + hh, i] - ctab_ref[2 * pair + hh, j + 1]) * LOG2E > floors[hh]
            for hh in range(2)])

    j_lo = lax.while_loop(lambda j: jnp.logical_and(j > 0, needed(j - 1)), lambda j: j - 1, i)
    lax.fori_loop(0, i - j_lo, lambda u, carry: (step(i - 1 - u), carry)[1], 0)

    out_t = jnp.concatenate(
        [acc_ref[hh, :HEAD_DIM, :] / acc_ref[hh, HEAD_DIM:HEAD_DIM + 1, :] for hh in range(2)],
        axis=0)
    o_ref[...] = jnp.transpose(out_t)


def _attn_call(ctab, qn, kn, q, k, v):
    d_attn, s = v.shape
    t = ATTN_TILE
    pairs = d_attn // LANES
    return pl.pallas_call(
        _attn_kernel,
        grid=(pairs, s // t),
        in_specs=[pl.BlockSpec(memory_space=pltpu.SMEM)] * 3
                 + [pl.BlockSpec((t, 2 * LANES), lambda hp, i: (i, hp)),
                    pl.BlockSpec((s, 2 * LANES), lambda hp, i: (0, hp)),
                    pl.BlockSpec((LANES, s), lambda hp, i: (hp, 0))],
        out_specs=pl.BlockSpec((t, LANES), lambda hp, i: (i, hp)),
        out_shape=jax.ShapeDtypeStruct((s, d_attn), F32),
        scratch_shapes=[pltpu.VMEM((2, 1, t), F32),
                        pltpu.VMEM((2, HEAD_DIM + ONES_ROWS, t), F32),
                        pltpu.VMEM((2, t, t), F32), pltpu.VMEM((2, 1, t), F32),
                        pltpu.SMEM((2, s // t), F32)],
        compiler_params=pltpu.CompilerParams(
            dimension_semantics=("arbitrary", "arbitrary"), vmem_limit_bytes=VMEM_LIMIT),
        name="fox_attn",
    )(ctab, qn, kn, q, k, v)


def _post_kernel(x_ref, attn_ref, sgu_ref, oga_ref, wout_ref, fg_ref, wgu_ref, wdown_ref,
                 fin_ref, o_ref, act_ref, *, final_norm):
    d_ff = wdown_ref.shape[0]
    an = (_rms_scale(attn_ref[...]) * oga_ref[...]).astype(BF16)
    merged = jnp.concatenate([an, sgu_ref[...]], axis=-1)
    x1 = x_ref[...] + jnp.dot(merged, wout_ref[...], preferred_element_type=F32)

    xb = (_rms_scale(x1) * fg_ref[...]).astype(BF16)
    for c in range(d_ff // FF_CHUNK):
        lo = c * FF_CHUNK
        gate = jnp.dot(xb, wgu_ref[:, lo:lo + FF_CHUNK], preferred_element_type=F32)
        up = jnp.dot(xb, wgu_ref[:, d_ff + lo:d_ff + lo + FF_CHUNK], preferred_element_type=F32)
        act_ref[:, lo:lo + FF_CHUNK] = (gate * jax.nn.sigmoid(gate) * up).astype(BF16)
    x2 = x1 + jnp.dot(act_ref[...], wdown_ref[...], preferred_element_type=F32)
    if final_norm:
        x2 = _rms_scale(x2) * fin_ref[...]
    o_ref[...] = x2


def _post_call(layer, x, attn, sgu, oga, wout, fg, wgu, wdown, fin, final_norm):
    s, d = x.shape
    rows = POST_ROWS
    d_ff = wdown.shape[1]
    row_blk = lambda w: pl.BlockSpec((rows, w), lambda i: (i, 0))
    return pl.pallas_call(
        functools.partial(_post_kernel, final_norm=final_norm),
        grid=(s // rows,),
        in_specs=[row_blk(d), row_blk(attn.shape[1]), row_blk(sgu.shape[1]),
                  _resident(oga.shape), _layer_resident(wout, layer), _resident(fg.shape),
                  _layer_resident(wgu, layer), _layer_resident(wdown, layer),
                  _resident(fin.shape)],
        out_specs=row_blk(d),
        out_shape=jax.ShapeDtypeStruct((s, d), F32),
        scratch_shapes=[pltpu.VMEM((rows, d_ff), BF16)],
        compiler_params=pltpu.CompilerParams(
            dimension_semantics=("arbitrary",), vmem_limit_bytes=VMEM_LIMIT),
        name="fox_post",
    )(x, attn, sgu, oga, wout, fg, wgu, wdown, fin)


def kernel(x, mix_norm_g, w_in, b_f, sgu_ln_g, sgu_ln_b, w_s, b_s, out_norm_g, w_out,
           ffn_norm_g, w_gate_up, w_down, final_norm_g):
    batch, seq, d_model = x.shape
    depth = w_in.shape[0]
    d_attn = N_HEADS * HEAD_DIM
    d_sgu = N_GROUPS * GROUP_DIM
    assert batch == 1 and w_in.shape[2] == 3 * d_attn + N_HEADS + 2 * d_sgu
    assert seq % max(PROJ_ROWS, POST_ROWS) == 0 and PROJ_ROWS == ATTN_TILE
    assert w_down.shape[1] % FF_CHUNK == 0 and w_s.shape[2] == SGU_CHUNK

    row = lambda a: a.reshape(1, -1).astype(F32)
    xs = x.reshape(seq, d_model).astype(F32)
    n_qkvf = 3 * d_attn + N_HEADS
    win = jnp.concatenate(
        [w_in[:, :, :n_qkvf], jnp.zeros((depth, d_model, F_PAD - N_HEADS), w_in.dtype),
         w_in[:, :, n_qkvf:]], axis=2).astype(BF16)
    wout, wgu, wdown = w_out.astype(BF16), w_gate_up.astype(BF16), w_down.astype(BF16)
    for l in range(depth):
        bs_pair = jnp.repeat(
            b_s[l].reshape(N_GROUPS // 2, 2, SGU_CHUNK).transpose(0, 2, 1), GROUP_DIM, axis=2)
        q, k, v, c_t, norms, sgu = _proj_call(
            l, xs, row(mix_norm_g[l]), win, b_f[l].reshape(N_HEADS, 1).astype(F32),
            row(sgu_ln_g[l]), row(sgu_ln_b[l]), w_s[l].astype(F32), bs_pair.astype(F32),
            row(out_norm_g[l, d_attn:]))
        tabs = norms[:, :2, :N_HEADS].transpose(1, 2, 0)
        attn = _attn_call(c_t[:, ::ATTN_TILE], tabs[0], tabs[1], q, k, v)
        xs = _post_call(
            l, xs, attn, sgu, row(out_norm_g[l, :d_attn]), wout, row(ffn_norm_g[l]), wgu, wdown,
            row(final_norm_g), final_norm=(l == depth - 1))
    return xs.reshape(batch, seq, d_model).astype(x.dtype)
```

```python
import functools
import math

import jax
import jax.numpy as jnp
from jax import lax
from jax.experimental import pallas as pl
from jax.experimental.pallas import tpu as pltpu

F32 = jnp.float32
BF16 = jnp.bfloat16

EPS = 1e-6
HEAD_DIM = 64
N_HEADS = 8
N_GROUPS = 8
GROUP_DIM = 64
SGU_CHUNK = 128
STREAM_CHUNK = 64
LANES = 128
F_PAD = LANES

PROJ_ROWS = 512
ATTN_TILE = 512
ATTN_TILES_PER_STEP = 4
POST_ROWS = 512
FF_CHUNK = 256
LOG2E = math.log2(math.e)
QK_SCALE = HEAD_DIM ** -0.5 * LOG2E
NORM_SLACK = 1.01
PRUNE_BITS = 40.0
ONES_ROWS = 16
FEAT_STRIDE = 8
VMEM_LIMIT = 56 * 1024 * 1024


def _rms_scale(x):
    return x * lax.rsqrt(jnp.mean(x * x, axis=-1, keepdims=True) + EPS)


def _resident(shape):
    zeros = (0,) * len(shape)
    return pl.BlockSpec(shape, lambda *_: zeros, pipeline_mode=pl.Buffered(1))


def _layer_resident(stacked, layer):
    tail = (0,) * (stacked.ndim - 1)
    return pl.BlockSpec((None,) + stacked.shape[1:], lambda *_: (layer,) + tail,
                        pipeline_mode=pl.Buffered(1))


def _split3(a):
    hi = a.astype(BF16).astype(F32)
    r = a - hi
    mid = r.astype(BF16).astype(F32)
    lo = (r - mid).astype(BF16).astype(F32)
    return hi, mid, lo


def _proj_kernel(x_ref, g_ref, win_ref, bf_ref, lng_ref, lnb_ref, ws_ref, bs_ref,
                 og_ref, q_ref, k_ref, v_ref, ct_ref, nrm_ref, sgu_ref, carry_ref, mix_ref,
                 h_ref, z_ref, zu_ref, zvb_ref):
    rows = x_ref.shape[0]
    d_attn = v_ref.shape[0]
    d_sgu = sgu_ref.shape[1]
    step = pl.program_id(0)
    n_h = h_ref.shape[2]

    @pl.when(step == 0)
    def _():
        h_ref[1] = jnp.zeros(h_ref.shape[1:], F32)
        z_ref[1] = jnp.zeros(z_ref.shape[1:], F32)

    @pl.when(step <= 1)
    def _():
        carry_ref[...] = jnp.zeros_like(carry_ref)

    def work(cur, prev):
        xb = (_rms_scale(x_ref[...]) * g_ref[...]).astype(BF16)
        z_ref[cur] = jnp.dot(xb, win_ref[:, n_h:], preferred_element_type=F32)

        z = z_ref[prev]
        gz = 0.5 * z * (1.0 + lax.erf(z * (2.0 ** -0.5)))
        zv = gz[:, d_sgu:]
        mu = jnp.mean(zv, axis=-1, keepdims=True)
        zc = zv - mu
        zvn = zc * lax.rsqrt(jnp.mean(zc * zc, axis=-1, keepdims=True) + EPS)
        zu_ref[...] = gz[:, :d_sgu]
        zvb_ref[...] = (zvn * lng_ref[...] + lnb_ref[...]).astype(BF16)

        h_ref[cur] = jnp.dot(xb, win_ref[:, :n_h], preferred_element_type=F32)

        h = h_ref[prev]
        for blk in range(d_attn // LANES):
            lo = 2 * d_attn + blk * LANES
            v_ref[blk * LANES:(blk + 1) * LANES, :] = jnp.transpose(h[:, lo:lo + LANES]).astype(BF16)

        f_t = jnp.transpose(h[:, 3 * d_attn:])[:N_HEADS, :] + bf_ref[...]
        log_f = jnp.minimum(f_t, 0.0) - jnp.log1p(jnp.exp(-jnp.abs(f_t)))
        lane = lax.broadcasted_iota(jnp.int32, log_f.shape, 1)
        c = log_f
        shift = 1
        while shift < rows:
            c = c + jnp.where(lane >= shift, pltpu.roll(c, shift, axis=1), 0.0)
            shift *= 2
        c = c + carry_ref[:, 0:1]
        ct_ref[...] = c
        carry_ref[...] = jnp.broadcast_to(c[:, rows - 1:rows], carry_ref.shape)

        a = jnp.concatenate(
            [c[:, u:u + ATTN_TILE] - c[:, u:u + 1] for u in range(0, rows, ATTN_TILE)], axis=1)
        parts = _split3(a * LOG2E)
        sub = lax.broadcasted_iota(jnp.int32, (8, rows), 0)

        def feature_rows(head):
            pick = lambda part: jnp.broadcast_to(part[head:head + 1, :], (8, rows))
            return jnp.where(sub == 0, pick(parts[0]),
                             jnp.where(sub == 1, pick(parts[1]),
                                       jnp.where(sub == 2, pick(parts[2]), 0.0)))

        lane = lax.broadcasted_iota(jnp.int32, (rows, LANES), 1)
        lane1 = lax.broadcasted_iota(jnp.int32, (1, LANES), 1)
        pad = jnp.zeros((HEAD_DIM - 2 * FEAT_STRIDE, rows), F32)
        head_sel = (lax.broadcasted_iota(jnp.int32, (d_attn, LANES), 0) // HEAD_DIM
                    == lax.broadcasted_iota(jnp.int32, (d_attn, LANES), 1)).astype(BF16)

        def max_row_norms(block):
            b16 = block.astype(BF16)
            nsq = jnp.dot(b16 * b16, head_sel, preferred_element_type=F32)
            return jnp.sqrt(jnp.max(nsq, axis=0, keepdims=True) * NORM_SLACK)

        nrm_ref[0, 0:1, :] = max_row_norms(h[:, :d_attn] * QK_SCALE)
        nrm_ref[0, 1:2, :] = max_row_norms(h[:, d_attn:2 * d_attn])
        nrm_ref[0, 2:, :] = jnp.zeros((N_HEADS - 2, LANES), F32)

        for pair in range(N_HEADS // 2):
            f_even, f_odd = feature_rows(2 * pair), feature_rows(2 * pair + 1)
            feats = jnp.transpose(
                jnp.concatenate([f_odd, f_odd, pad, f_even, f_even, pad], axis=0))
            blk = slice(pair * LANES, (pair + 1) * LANES)
            qb = h[:, blk] * QK_SCALE
            kb = h[:, d_attn + pair * LANES:d_attn + (pair + 1) * LANES]
            for odd in (0, 1):
                base = 0 if odd else HEAD_DIM
                own = (lane >= HEAD_DIM) if odd else (lane < HEAD_DIM)
                first = lane < base + FEAT_STRIDE
                in_first = (lane1 >= base) & (lane1 < base + 3)
                in_second = (lane1 >= base + FEAT_STRIDE) & (lane1 < base + FEAT_STRIDE + 3)
                q_aug = jnp.where(own, qb,
                                  jnp.where(first, feats, jnp.where(in_second, -1.0, 0.0)))
                k_aug = jnp.where(own, kb,
                                  jnp.where(first, jnp.where(in_first, 1.0, 0.0), feats))
                dst = slice((2 * pair + odd) * LANES, (2 * pair + odd + 1) * LANES)
                q_ref[:, dst] = q_aug.astype(BF16)
                k_ref[:, dst] = k_aug.astype(BF16)

        ri = lax.broadcasted_iota(jnp.int32, (SGU_CHUNK, SGU_CHUNK), 0) // STREAM_CHUNK
        ci = lax.broadcasted_iota(jnp.int32, (SGU_CHUNK, SGU_CHUNK), 1) // STREAM_CHUNK
        w_mask = (ci <= ri).astype(F32)
        wm = (ws_ref[...] * w_mask[None]).astype(BF16)
        low_half = lax.broadcasted_iota(jnp.int32, (SGU_CHUNK, LANES), 1) < GROUP_DIM
        for pair in range(N_GROUPS // 2):
            w_pair = wm[2 * pair:2 * pair + 2].reshape(2 * SGU_CHUNK, SGU_CHUNK)
            cols = slice(pair * LANES, (pair + 1) * LANES)
            for chunk in range(rows // SGU_CHUNK):
                rws = slice(chunk * SGU_CHUNK, (chunk + 1) * SGU_CHUNK)
                both = jnp.dot(w_pair, zvb_ref[rws, cols], preferred_element_type=F32)
                mixed = jnp.where(low_half, both[:SGU_CHUNK], both[SGU_CHUNK:]) + bs_ref[pair]
                mix_ref[rws, cols] = zu_ref[rws, cols] * mixed
        sgu_ref[...] = (_rms_scale(mix_ref[...]) * og_ref[...]).astype(BF16)

    pl.when(step % 2 == 0)(lambda: work(0, 1))
    pl.when(step % 2 == 1)(lambda: work(1, 0))


def _proj_call(layer, x, g, win, bf, lng, lnb, ws, bs_pair, og):
    s, d = x.shape
    d_attn, d_sgu = N_HEADS * HEAD_DIM, N_GROUPS * GROUP_DIM
    n_h = 3 * d_attn + F_PAD
    rows = PROJ_ROWS
    tiles = s // rows
    done = lambda i: jnp.maximum(i - 1, 0)
    row_blk = lambda w: pl.BlockSpec((rows, w), lambda i: (done(i), 0))
    return pl.pallas_call(
        _proj_kernel,
        grid=(tiles + 1,),
        in_specs=[pl.BlockSpec((rows, d), lambda i: (jnp.minimum(i, tiles - 1), 0)),
                  _resident(g.shape), _layer_resident(win, layer), _resident(bf.shape),
                  _resident(lng.shape), _resident(lnb.shape),
                  _resident(ws.shape), _resident(bs_pair.shape), _resident(og.shape)],
        out_specs=[row_blk(N_HEADS * LANES), row_blk(N_HEADS * LANES),
                   pl.BlockSpec((d_attn, rows), lambda i: (0, done(i))),
                   pl.BlockSpec((N_HEADS, rows), lambda i: (0, done(i))),
                   pl.BlockSpec((1, N_HEADS, LANES), lambda i: (done(i), 0, 0)), row_blk(d_sgu)],
        out_shape=[jax.ShapeDtypeStruct((s, N_HEADS * LANES), BF16)] * 2
        + [jax.ShapeDtypeStruct((d_attn, s), BF16),
           jax.ShapeDtypeStruct((N_HEADS, s), F32),
           jax.ShapeDtypeStruct((tiles, N_HEADS, LANES), F32),
           jax.ShapeDtypeStruct((s, d_sgu), BF16)],
        scratch_shapes=[pltpu.VMEM((N_HEADS, LANES), F32), pltpu.VMEM((rows, d_sgu), F32),
                        pltpu.VMEM((2, rows, n_h), F32),
                        pltpu.VMEM((2, rows, 2 * d_sgu), F32),
                        pltpu.VMEM((rows, d_sgu), F32), pltpu.VMEM((rows, d_sgu), BF16)],
        compiler_params=pltpu.CompilerParams(
            dimension_semantics=("arbitrary",), vmem_limit_bytes=VMEM_LIMIT),
        name="fox_proj",
    )(x, g, win, bf, lng, lnb, ws, bs_pair, og)


def _attn_kernel(ctab_ref, qn_ref, kn_ref, q_ref, k_ref, v_ref, o_ref, m_ref, acc_ref, s_ref,
                 smax_ref):
    t = ATTN_TILE
    pair = pl.program_id(0)
    lax.fori_loop(0, q_ref.shape[0] // t, functools.partial(
        _attn_query_tile, ctab_ref, qn_ref, kn_ref, q_ref, k_ref, v_ref, o_ref, m_ref, acc_ref,
        s_ref, smax_ref, pair, pl.program_id(1) * (q_ref.shape[0] // t)), 0)


def _attn_query_tile(ctab_ref, qn_ref, kn_ref, q_ref, k_ref, v_ref, o_ref, m_ref, acc_ref, s_ref,
                     smax_ref, pair, first_tile, sub, carry):
    t = ATTN_TILE
    i = first_tile + sub
    q_rows = pl.ds(pl.multiple_of(sub * t, t), t)

    key = lax.broadcasted_iota(jnp.int32, (t, t), 0)
    qry = lax.broadcasted_iota(jnp.int32, (t, t), 1)
    ones = jnp.ones((ONES_ROWS, t), BF16)
    nt_dims = (((1,), (1,)), ((), ()))

    def scores_t(hh, start):
        hl = slice(hh * LANES, (hh + 1) * LANES)
        return lax.dot_general(k_ref[pl.ds(start, t), hl], q_ref[q_rows, hl], nt_dims,
                               preferred_element_type=F32)

    def weighted_values_t(hh, p, start):
        v_t = v_ref[hh * HEAD_DIM:(hh + 1) * HEAD_DIM, pl.ds(start, t)]
        v_ext = jnp.concatenate([v_t, ones], axis=0)
        return jnp.dot(v_ext, p.astype(BF16), preferred_element_type=F32)

    def tile_start(j):
        return pl.multiple_of(jnp.maximum(j, 0) * t, t)

    def put_scores(hh, s, diagonal):
        if diagonal:
            s = jnp.where(key <= qry, s, -jnp.inf)
        s_ref[hh] = s
        smax_ref[hh] = jnp.max(s, axis=0, keepdims=True)

    def weights(hh, j):
        head = 2 * pair + hh
        shift = (ctab_ref[head, i] - ctab_ref[head, j]) * LOG2E
        m_old = m_ref[hh]
        m_new = jnp.maximum(m_old, smax_ref[hh] + shift)
        m_ref[hh] = m_new
        return jnp.exp2(s_ref[hh] - (m_new - shift)), jnp.exp2(m_old - m_new)

    def accumulate(hh, j, p, alpha):
        acc_ref[hh] = alpha * acc_ref[hh] + weighted_values_t(hh, p, tile_start(j))

    def step(j, diagonal):
        put_scores(1, scores_t(1, tile_start(j)), diagonal)
        p0, alpha0 = weights(0, j)
        s_next = scores_t(0, tile_start(j - 1))
        accumulate(0, j, p0, alpha0)
        put_scores(0, s_next, False)
        p1, alpha1 = weights(1, j)
        accumulate(1, j, p1, alpha1)

    def largest_logit(head):
        kmax = lax.fori_loop(0, i + 1, lambda j, m: jnp.maximum(m, kn_ref[head, j]), 0.0)
        return qn_ref[head, i] * kmax

    logit_cap = [largest_logit(2 * pair + hh) for hh in range(2)]

    m_ref[...] = jnp.full(m_ref.shape, -jnp.inf, F32)
    acc_ref[...] = jnp.zeros(acc_ref.shape, F32)
    put_scores(0, scores_t(0, tile_start(i)), True)
    step(i, True)

    m_low = jnp.min(m_ref[...], axis=-1)
    margins = [logit_cap[hh] - m_low[hh, 0] + PRUNE_BITS for hh in range(2)]

    def needed(j):
        return functools.reduce(jnp.logical_or, [
            (ctab_ref[2 * pair + hh, i] - ctab_ref[2 * pair + hh, j + 1]) * LOG2E > -margins[hh]
            for hh in range(2)])

    j_lo = lax.while_loop(lambda j: jnp.logical_and(j > 0, needed(j - 1)), lambda j: j - 1, i)
    lax.fori_loop(0, i - j_lo, lambda u, carry: (step(i - 1 - u, False), carry)[1], 0)

    out_t = jnp.concatenate(
        [acc_ref[hh, :HEAD_DIM, :] / acc_ref[hh, HEAD_DIM:HEAD_DIM + 1, :] for hh in range(2)],
        axis=0)
    o_ref[q_rows, :] = jnp.transpose(out_t)
    return carry


def _attn_call(ctab, qn, kn, q, k, v):
    d_attn, s = v.shape
    t = ATTN_TILE
    tq = t * ATTN_TILES_PER_STEP
    pairs = d_attn // LANES
    return pl.pallas_call(
        _attn_kernel,
        grid=(pairs, s // tq),
        in_specs=[pl.BlockSpec(memory_space=pltpu.SMEM)] * 3
                 + [pl.BlockSpec((tq, 2 * LANES), lambda hp, i: (i, hp)),
                    pl.BlockSpec((s, 2 * LANES), lambda hp, i: (0, hp)),
                    pl.BlockSpec((LANES, s), lambda hp, i: (hp, 0))],
        out_specs=pl.BlockSpec((tq, LANES), lambda hp, i: (i, hp)),
        out_shape=jax.ShapeDtypeStruct((s, d_attn), F32),
        scratch_shapes=[pltpu.VMEM((2, 1, t), F32),
                        pltpu.VMEM((2, HEAD_DIM + ONES_ROWS, t), F32),
                        pltpu.VMEM((2, t, t), F32), pltpu.VMEM((2, 1, t), F32)],
        compiler_params=pltpu.CompilerParams(
            dimension_semantics=("arbitrary", "arbitrary"), vmem_limit_bytes=VMEM_LIMIT),
        name="fox_attn",
    )(ctab, qn, kn, q, k, v)


def _post_kernel(x_ref, attn_ref, sgu_ref, oga_ref, wout_ref, fg_ref, wgu_ref, wdown_ref,
                 fin_ref, o_ref, act_ref, *, final_norm):
    d_ff = wdown_ref.shape[0]
    an = (_rms_scale(attn_ref[...]) * oga_ref[...]).astype(BF16)
    merged = jnp.concatenate([an, sgu_ref[...]], axis=-1)
    x1 = x_ref[...] + jnp.dot(merged, wout_ref[...], preferred_element_type=F32)

    xb = (_rms_scale(x1) * fg_ref[...]).astype(BF16)
    for c in range(d_ff // FF_CHUNK):
        lo = c * FF_CHUNK
        gate = jnp.dot(xb, wgu_ref[:, lo:lo + FF_CHUNK], preferred_element_type=F32)
        up = jnp.dot(xb, wgu_ref[:, d_ff + lo:d_ff + lo + FF_CHUNK], preferred_element_type=F32)
        act_ref[:, lo:lo + FF_CHUNK] = (gate * jax.nn.sigmoid(gate) * up).astype(BF16)
    x2 = x1 + jnp.dot(act_ref[...], wdown_ref[...], preferred_element_type=F32)
    if final_norm:
        x2 = _rms_scale(x2) * fin_ref[...]
    o_ref[...] = x2


def _post_call(layer, x, attn, sgu, oga, wout, fg, wgu, wdown, fin, final_norm):
    s, d = x.shape
    rows = POST_ROWS
    d_ff = wdown.shape[1]
    row_blk = lambda w: pl.BlockSpec((rows, w), lambda i: (i, 0))
    return pl.pallas_call(
        functools.partial(_post_kernel, final_norm=final_norm),
        grid=(s // rows,),
        in_specs=[row_blk(d), row_blk(attn.shape[1]), row_blk(sgu.shape[1]),
                  _resident(oga.shape), _layer_resident(wout, layer), _resident(fg.shape),
                  _layer_resident(wgu, layer), _layer_resident(wdown, layer),
                  _resident(fin.shape)],
        out_specs=row_blk(d),
        out_shape=jax.ShapeDtypeStruct((s, d), F32),
        scratch_shapes=[pltpu.VMEM((rows, d_ff), BF16)],
        compiler_params=pltpu.CompilerParams(
            dimension_semantics=("arbitrary",), vmem_limit_bytes=VMEM_LIMIT),
        name="fox_post",
    )(x, attn, sgu, oga, wout, fg, wgu, wdown, fin)


def kernel(x, mix_norm_g, w_in, b_f, sgu_ln_g, sgu_ln_b, w_s, b_s, out_norm_g, w_out,
           ffn_norm_g, w_gate_up, w_down, final_norm_g):
    batch, seq, d_model = x.shape
    depth = w_in.shape[0]
    d_attn = N_HEADS * HEAD_DIM
    d_sgu = N_GROUPS * GROUP_DIM
    assert batch == 1 and w_in.shape[2] == 3 * d_attn + N_HEADS + 2 * d_sgu
    assert seq % max(PROJ_ROWS, POST_ROWS, ATTN_TILE * ATTN_TILES_PER_STEP) == 0
    assert PROJ_ROWS == ATTN_TILE
    assert w_down.shape[1] % FF_CHUNK == 0 and w_s.shape[2] == SGU_CHUNK

    row = lambda a: a.reshape(1, -1).astype(F32)
    xs = x.reshape(seq, d_model).astype(F32)
    n_qkvf = 3 * d_attn + N_HEADS
    win = jnp.concatenate(
        [w_in[:, :, :n_qkvf], jnp.zeros((depth, d_model, F_PAD - N_HEADS), w_in.dtype),
         w_in[:, :, n_qkvf:]], axis=2).astype(BF16)
    wout, wgu, wdown = w_out.astype(BF16), w_gate_up.astype(BF16), w_down.astype(BF16)
    for l in range(depth):
        bs_pair = jnp.repeat(
            b_s[l].reshape(N_GROUPS // 2, 2, SGU_CHUNK).transpose(0, 2, 1), GROUP_DIM, axis=2)
        q, k, v, c_t, norms, sgu = _proj_call(
            l, xs, row(mix_norm_g[l]), win, b_f[l].reshape(N_HEADS, 1).astype(F32),
            row(sgu_ln_g[l]), row(sgu_ln_b[l]), w_s[l].astype(F32), bs_pair.astype(F32),
            row(out_norm_g[l, d_attn:]))
        tabs = norms[:, :2, :N_HEADS].transpose(1, 2, 0)
        attn = _attn_call(c_t[:, ::ATTN_TILE], tabs[0], tabs[1], q, k, v)
        xs = _post_call(
            l, xs, attn, sgu, row(out_norm_g[l, :d_attn]), wout, row(ffn_norm_g[l]), wgu, wdown,
            row(final_norm_g), final_norm=(l == depth - 1))
    return xs.reshape(batch, seq, d_model).astype(x.dtype)
```

```python
import functools
import math

import jax
import jax.numpy as jnp
from jax import lax
from jax.experimental import pallas as pl
from jax.experimental.pallas import tpu as pltpu

F32 = jnp.float32
BF16 = jnp.bfloat16

EPS = 1e-6
HEAD_DIM = 64
N_HEADS = 8
N_GROUPS = 8
GROUP_DIM = 64
SGU_CHUNK = 128
STREAM_CHUNK = 64
LANES = 128
F_PAD = LANES

PROJ_ROWS = 512
ATTN_TILE = 512
ATTN_TILES_PER_STEP = 8
POST_ROWS = 512
FF_CHUNK = 256
LOG2E = math.log2(math.e)
QK_SCALE = HEAD_DIM ** -0.5 * LOG2E
NORM_SLACK = 1.01
PRUNE_BITS = 40.0
ONES_ROWS = 16
FEAT_STRIDE = 8
VMEM_LIMIT = 56 * 1024 * 1024


def _rms_scale(x):
    return x * lax.rsqrt(jnp.mean(x * x, axis=-1, keepdims=True) + EPS)


def _resident(shape):
    zeros = (0,) * len(shape)
    return pl.BlockSpec(shape, lambda *_: zeros, pipeline_mode=pl.Buffered(1))


def _layer_resident(stacked, layer):
    tail = (0,) * (stacked.ndim - 1)
    return pl.BlockSpec((None,) + stacked.shape[1:], lambda *_: (layer,) + tail,
                        pipeline_mode=pl.Buffered(1))


def _split3(a):
    hi = a.astype(BF16).astype(F32)
    r = a - hi
    mid = r.astype(BF16).astype(F32)
    lo = (r - mid).astype(BF16).astype(F32)
    return hi, mid, lo


def _proj_kernel(x_ref, g_ref, win_ref, bf_ref, lng_ref, lnb_ref, ws_ref, bs_ref,
                 og_ref, q_ref, k_ref, v_ref, ct_ref, nrm_ref, sgu_ref, carry_ref, mix_ref,
                 h_ref, z_ref, zu_ref, zvb_ref):
    rows = x_ref.shape[0]
    d_attn = v_ref.shape[0]
    d_sgu = sgu_ref.shape[1]
    step = pl.program_id(0)
    n_h = h_ref.shape[2]

    @pl.when(step == 0)
    def _():
        h_ref[1] = jnp.zeros(h_ref.shape[1:], F32)
        z_ref[1] = jnp.zeros(z_ref.shape[1:], F32)

    @pl.when(step <= 1)
    def _():
        carry_ref[...] = jnp.zeros_like(carry_ref)

    def work(cur, prev):
        xb = (_rms_scale(x_ref[...]) * g_ref[...]).astype(BF16)
        z_ref[cur] = jnp.dot(xb, win_ref[:, n_h:], preferred_element_type=F32)

        z = z_ref[prev]
        gz = 0.5 * z * (1.0 + lax.erf(z * (2.0 ** -0.5)))
        zv = gz[:, d_sgu:]
        mu = jnp.mean(zv, axis=-1, keepdims=True)
        zc = zv - mu
        zvn = zc * lax.rsqrt(jnp.mean(zc * zc, axis=-1, keepdims=True) + EPS)
        zu_ref[...] = gz[:, :d_sgu]
        zvb_ref[...] = (zvn * lng_ref[...] + lnb_ref[...]).astype(BF16)

        h_ref[cur] = jnp.dot(xb, win_ref[:, :n_h], preferred_element_type=F32)

        h = h_ref[prev]
        for blk in range(d_attn // LANES):
            lo = 2 * d_attn + blk * LANES
            v_ref[blk * LANES:(blk + 1) * LANES, :] = jnp.transpose(h[:, lo:lo + LANES]).astype(BF16)

        f_t = jnp.transpose(h[:, 3 * d_attn:])[:N_HEADS, :] + bf_ref[...]
        log_f = jnp.minimum(f_t, 0.0) - jnp.log1p(jnp.exp(-jnp.abs(f_t)))
        lane = lax.broadcasted_iota(jnp.int32, log_f.shape, 1)
        c = log_f
        shift = 1
        while shift < rows:
            c = c + jnp.where(lane >= shift, pltpu.roll(c, shift, axis=1), 0.0)
            shift *= 2
        c = c + carry_ref[:, 0:1]
        ct_ref[...] = c
        carry_ref[...] = jnp.broadcast_to(c[:, rows - 1:rows], carry_ref.shape)

        a = jnp.concatenate(
            [c[:, u:u + ATTN_TILE] - c[:, u:u + 1] for u in range(0, rows, ATTN_TILE)], axis=1)
        parts = _split3(a * LOG2E)
        sub = lax.broadcasted_iota(jnp.int32, (8, rows), 0)

        def feature_rows(head):
            pick = lambda part: jnp.broadcast_to(part[head:head + 1, :], (8, rows))
            return jnp.where(sub == 0, pick(parts[0]),
                             jnp.where(sub == 1, pick(parts[1]),
                                       jnp.where(sub == 2, pick(parts[2]), 0.0)))

        lane = lax.broadcasted_iota(jnp.int32, (rows, LANES), 1)
        lane1 = lax.broadcasted_iota(jnp.int32, (1, LANES), 1)
        pad = jnp.zeros((HEAD_DIM - 2 * FEAT_STRIDE, rows), F32)
        head_sel = (lax.broadcasted_iota(jnp.int32, (d_attn, LANES), 0) // HEAD_DIM
                    == lax.broadcasted_iota(jnp.int32, (d_attn, LANES), 1)).astype(BF16)

        def max_row_norms(block):
            b16 = block.astype(BF16)
            nsq = jnp.dot(b16 * b16, head_sel, preferred_element_type=F32)
            return jnp.sqrt(jnp.max(nsq, axis=0, keepdims=True) * NORM_SLACK)

        nrm_ref[0, 0:1, :] = max_row_norms(h[:, :d_attn] * QK_SCALE)
        nrm_ref[0, 1:2, :] = max_row_norms(h[:, d_attn:2 * d_attn])
        nrm_ref[0, 2:, :] = jnp.zeros((N_HEADS - 2, LANES), F32)

        for pair in range(N_HEADS // 2):
            f_even, f_odd = feature_rows(2 * pair), feature_rows(2 * pair + 1)
            feats = jnp.transpose(
                jnp.concatenate([f_odd, f_odd, pad, f_even, f_even, pad], axis=0))
            blk = slice(pair * LANES, (pair + 1) * LANES)
            qb = h[:, blk] * QK_SCALE
            kb = h[:, d_attn + pair * LANES:d_attn + (pair + 1) * LANES]
            for odd in (0, 1):
                base = 0 if odd else HEAD_DIM
                own = (lane >= HEAD_DIM) if odd else (lane < HEAD_DIM)
                first = lane < base + FEAT_STRIDE
                in_first = (lane1 >= base) & (lane1 < base + 3)
                in_second = (lane1 >= base + FEAT_STRIDE) & (lane1 < base + FEAT_STRIDE + 3)
                q_aug = jnp.where(own, qb,
                                  jnp.where(first, feats, jnp.where(in_second, -1.0, 0.0)))
                k_aug = jnp.where(own, kb,
                                  jnp.where(first, jnp.where(in_first, 1.0, 0.0), feats))
                dst = slice((2 * pair + odd) * LANES, (2 * pair + odd + 1) * LANES)
                q_ref[:, dst] = q_aug.astype(BF16)
                k_ref[:, dst] = k_aug.astype(BF16)

        ri = lax.broadcasted_iota(jnp.int32, (SGU_CHUNK, SGU_CHUNK), 0) // STREAM_CHUNK
        ci = lax.broadcasted_iota(jnp.int32, (SGU_CHUNK, SGU_CHUNK), 1) // STREAM_CHUNK
        w_mask = (ci <= ri).astype(F32)
        wm = (ws_ref[...] * w_mask[None]).astype(BF16)
        low_half = lax.broadcasted_iota(jnp.int32, (SGU_CHUNK, LANES), 1) < GROUP_DIM
        for pair in range(N_GROUPS // 2):
            w_pair = wm[2 * pair:2 * pair + 2].reshape(2 * SGU_CHUNK, SGU_CHUNK)
            cols = slice(pair * LANES, (pair + 1) * LANES)
            for chunk in range(rows // SGU_CHUNK):
                rws = slice(chunk * SGU_CHUNK, (chunk + 1) * SGU_CHUNK)
                both = jnp.dot(w_pair, zvb_ref[rws, cols], preferred_element_type=F32)
                mixed = jnp.where(low_half, both[:SGU_CHUNK], both[SGU_CHUNK:]) + bs_ref[pair]
                mix_ref[rws, cols] = zu_ref[rws, cols] * mixed
        sgu_ref[...] = (_rms_scale(mix_ref[...]) * og_ref[...]).astype(BF16)

    pl.when(step % 2 == 0)(lambda: work(0, 1))
    pl.when(step % 2 == 1)(lambda: work(1, 0))


def _proj_call(layer, x, g, win, bf, lng, lnb, ws, bs_pair, og):
    s, d = x.shape
    d_attn, d_sgu = N_HEADS * HEAD_DIM, N_GROUPS * GROUP_DIM
    n_h = 3 * d_attn + F_PAD
    rows = PROJ_ROWS
    tiles = s // rows
    done = lambda i: jnp.maximum(i - 1, 0)
    row_blk = lambda w: pl.BlockSpec((rows, w), lambda i: (done(i), 0))
    return pl.pallas_call(
        _proj_kernel,
        grid=(tiles + 1,),
        in_specs=[pl.BlockSpec((rows, d), lambda i: (jnp.minimum(i, tiles - 1), 0)),
                  _resident(g.shape), _layer_resident(win, layer), _resident(bf.shape),
                  _resident(lng.shape), _resident(lnb.shape),
                  _resident(ws.shape), _resident(bs_pair.shape), _resident(og.shape)],
        out_specs=[row_blk(N_HEADS * LANES), row_blk(N_HEADS * LANES),
                   pl.BlockSpec((d_attn, rows), lambda i: (0, done(i))),
                   pl.BlockSpec((N_HEADS, rows), lambda i: (0, done(i))),
                   pl.BlockSpec((1, N_HEADS, LANES), lambda i: (done(i), 0, 0)), row_blk(d_sgu)],
        out_shape=[jax.ShapeDtypeStruct((s, N_HEADS * LANES), BF16)] * 2
        + [jax.ShapeDtypeStruct((d_attn, s), BF16),
           jax.ShapeDtypeStruct((N_HEADS, s), F32),
           jax.ShapeDtypeStruct((tiles, N_HEADS, LANES), F32),
           jax.ShapeDtypeStruct((s, d_sgu), BF16)],
        scratch_shapes=[pltpu.VMEM((N_HEADS, LANES), F32), pltpu.VMEM((rows, d_sgu), F32),
                        pltpu.VMEM((2, rows, n_h), F32),
                        pltpu.VMEM((2, rows, 2 * d_sgu), F32),
                        pltpu.VMEM((rows, d_sgu), F32), pltpu.VMEM((rows, d_sgu), BF16)],
        compiler_params=pltpu.CompilerParams(
            dimension_semantics=("arbitrary",), vmem_limit_bytes=VMEM_LIMIT),
        name="fox_proj",
    )(x, g, win, bf, lng, lnb, ws, bs_pair, og)


def _attn_kernel(ctab_ref, qn_ref, kn_ref, q_ref, k_ref, v_ref, o_ref, m_ref, acc_ref, s_ref,
                 smax_ref):
    t = ATTN_TILE
    pair = pl.program_id(0)
    lax.fori_loop(0, q_ref.shape[0] // t, functools.partial(
        _attn_query_tile, ctab_ref, qn_ref, kn_ref, q_ref, k_ref, v_ref, o_ref, m_ref, acc_ref,
        s_ref, smax_ref, pair, pl.program_id(1) * (q_ref.shape[0] // t)), 0)


def _attn_query_tile(ctab_ref, qn_ref, kn_ref, q_ref, k_ref, v_ref, o_ref, m_ref, acc_ref, s_ref,
                     smax_ref, pair, first_tile, sub, carry):
    t = ATTN_TILE
    i = first_tile + sub
    q_rows = pl.ds(pl.multiple_of(sub * t, t), t)

    key = lax.broadcasted_iota(jnp.int32, (t, t), 0)
    qry = lax.broadcasted_iota(jnp.int32, (t, t), 1)
    ones = jnp.ones((ONES_ROWS, t), BF16)
    nt_dims = (((1,), (1,)), ((), ()))

    def scores_t(hh, start):
        hl = slice(hh * LANES, (hh + 1) * LANES)
        return lax.dot_general(k_ref[pl.ds(start, t), hl], q_ref[q_rows, hl], nt_dims,
                               preferred_element_type=F32)

    def weighted_values_t(hh, p, start):
        v_t = v_ref[hh * HEAD_DIM:(hh + 1) * HEAD_DIM, pl.ds(start, t)]
        v_ext = jnp.concatenate([v_t, ones], axis=0)
        return jnp.dot(v_ext, p.astype(BF16), preferred_element_type=F32)

    def tile_start(j):
        return pl.multiple_of(jnp.maximum(j, 0) * t, t)

    def put_scores(hh, s, diagonal):
        if diagonal:
            s = jnp.where(key <= qry, s, -jnp.inf)
        s_ref[hh] = s
        smax_ref[hh] = jnp.max(s, axis=0, keepdims=True)

    def weights(hh, j):
        head = 2 * pair + hh
        shift = (ctab_ref[head, i] - ctab_ref[head, j]) * LOG2E
        m_old = m_ref[hh]
        m_new = jnp.maximum(m_old, smax_ref[hh] + shift)
        m_ref[hh] = m_new
        return jnp.exp2(s_ref[hh] - (m_new - shift)), jnp.exp2(m_old - m_new)

    def accumulate(hh, j, p, alpha):
        acc_ref[hh] = alpha * acc_ref[hh] + weighted_values_t(hh, p, tile_start(j))

    def step(j, diagonal):
        put_scores(1, scores_t(1, tile_start(j)), diagonal)
        p0, alpha0 = weights(0, j)
        s_next = scores_t(0, tile_start(j - 1))
        accumulate(0, j, p0, alpha0)
        put_scores(0, s_next, False)
        p1, alpha1 = weights(1, j)
        accumulate(1, j, p1, alpha1)

    def largest_logit(head):
        kmax = lax.fori_loop(0, i + 1, lambda j, m: jnp.maximum(m, kn_ref[head, j]), 0.0)
        return qn_ref[head, i] * kmax

    logit_cap = [largest_logit(2 * pair + hh) for hh in range(2)]

    m_ref[...] = jnp.full(m_ref.shape, -jnp.inf, F32)
    acc_ref[...] = jnp.zeros(acc_ref.shape, F32)
    put_scores(0, scores_t(0, tile_start(i)), True)
    step(i, True)

    m_low = jnp.min(m_ref[...], axis=-1)
    margins = [logit_cap[hh] - m_low[hh, 0] + PRUNE_BITS for hh in range(2)]

    def needed(j):
        return functools.reduce(jnp.logical_or, [
            (ctab_ref[2 * pair + hh, i] - ctab_ref[2 * pair + hh, j + 1]) * LOG2E > -margins[hh]
            for hh in range(2)])

    j_lo = lax.while_loop(lambda j: jnp.logical_and(j > 0, needed(j - 1)), lambda j: j - 1, i)
    lax.fori_loop(0, i - j_lo, lambda u, carry: (step(i - 1 - u, False), carry)[1], 0)

    out_t = jnp.concatenate(
        [acc_ref[hh, :HEAD_DIM, :] / acc_ref[hh, HEAD_DIM:HEAD_DIM + 1, :] for hh in range(2)],
        axis=0)
    o_ref[q_rows, :] = jnp.transpose(out_t)
    return carry


def _attn_call(ctab, qn, kn, q, k, v):
    d_attn, s = v.shape
    t = ATTN_TILE
    tq = t * ATTN_TILES_PER_STEP
    pairs = d_attn // LANES
    return pl.pallas_call(
        _attn_kernel,
        grid=(pairs, s // tq),
        in_specs=[pl.BlockSpec(memory_space=pltpu.SMEM)] * 3
                 + [pl.BlockSpec((tq, 2 * LANES), lambda hp, i: (i, hp)),
                    pl.BlockSpec((s, 2 * LANES), lambda hp, i: (0, hp)),
                    pl.BlockSpec((LANES, s), lambda hp, i: (hp, 0))],
        out_specs=pl.BlockSpec((tq, LANES), lambda hp, i: (i, hp)),
        out_shape=jax.ShapeDtypeStruct((s, d_attn), F32),
        scratch_shapes=[pltpu.VMEM((2, 1, t), F32),
                        pltpu.VMEM((2, HEAD_DIM + ONES_ROWS, t), F32),
                        pltpu.VMEM((2, t, t), F32), pltpu.VMEM((2, 1, t), F32)],
        compiler_params=pltpu.CompilerParams(
            dimension_semantics=("arbitrary", "arbitrary"), vmem_limit_bytes=VMEM_LIMIT),
        name="fox_attn",
    )(ctab, qn, kn, q, k, v)


def _post_kernel(x_ref, attn_ref, sgu_ref, oga_ref, wout_ref, fg_ref, wgu_ref, wdown_ref,
                 fin_ref, o_ref, act_ref, *, final_norm):
    d_ff = wdown_ref.shape[0]
    an = (_rms_scale(attn_ref[...]) * oga_ref[...]).astype(BF16)
    merged = jnp.concatenate([an, sgu_ref[...]], axis=-1)
    x1 = x_ref[...] + jnp.dot(merged, wout_ref[...], preferred_element_type=F32)

    xb = (_rms_scale(x1) * fg_ref[...]).astype(BF16)
    for c in range(d_ff // FF_CHUNK):
        lo = c * FF_CHUNK
        gate = jnp.dot(xb, wgu_ref[:, lo:lo + FF_CHUNK], preferred_element_type=F32)
        up = jnp.dot(xb, wgu_ref[:, d_ff + lo:d_ff + lo + FF_CHUNK], preferred_element_type=F32)
        act_ref[:, lo:lo + FF_CHUNK] = (gate * jax.nn.sigmoid(gate) * up).astype(BF16)
    x2 = x1 + jnp.dot(act_ref[...], wdown_ref[...], preferred_element_type=F32)
    if final_norm:
        x2 = _rms_scale(x2) * fin_ref[...]
    o_ref[...] = x2


def _post_call(layer, x, attn, sgu, oga, wout, fg, wgu, wdown, fin, final_norm):
    s, d = x.shape
    rows = POST_ROWS
    d_ff = wdown.shape[1]
    row_blk = lambda w: pl.BlockSpec((rows, w), lambda i: (i, 0))
    return pl.pallas_call(
        functools.partial(_post_kernel, final_norm=final_norm),
        grid=(s // rows,),
        in_specs=[row_blk(d), row_blk(attn.shape[1]), row_blk(sgu.shape[1]),
                  _resident(oga.shape), _layer_resident(wout, layer), _resident(fg.shape),
                  _layer_resident(wgu, layer), _layer_resident(wdown, layer),
                  _resident(fin.shape)],
        out_specs=row_blk(d),
        out_shape=jax.ShapeDtypeStruct((s, d), F32),
        scratch_shapes=[pltpu.VMEM((rows, d_ff), BF16)],
        compiler_params=pltpu.CompilerParams(
            dimension_semantics=("arbitrary",), vmem_limit_bytes=VMEM_LIMIT),
        name="fox_post",
    )(x, attn, sgu, oga, wout, fg, wgu, wdown, fin)


def kernel(x, mix_norm_g, w_in, b_f, sgu_ln_g, sgu_ln_b, w_s, b_s, out_norm_g, w_out,
           ffn_norm_g, w_gate_up, w_down, final_norm_g):
    batch, seq, d_model = x.shape
    depth = w_in.shape[0]
    d_attn = N_HEADS * HEAD_DIM
    d_sgu = N_GROUPS * GROUP_DIM
    assert batch == 1 and w_in.shape[2] == 3 * d_attn + N_HEADS + 2 * d_sgu
    assert seq % max(PROJ_ROWS, POST_ROWS, ATTN_TILE * ATTN_TILES_PER_STEP) == 0
    assert PROJ_ROWS == ATTN_TILE
    assert w_down.shape[1] % FF_CHUNK == 0 and w_s.shape[2] == SGU_CHUNK

    row = lambda a: a.reshape(1, -1).astype(F32)
    xs = x.reshape(seq, d_model).astype(F32)
    n_qkvf = 3 * d_attn + N_HEADS
    gap = F_PAD - N_HEADS
    win = (jnp.pad(w_in[:, :, :n_qkvf], ((0, 0), (0, 0), (0, gap + 2 * d_sgu))).astype(BF16)
           + jnp.pad(w_in[:, :, n_qkvf:], ((0, 0), (0, 0), (n_qkvf + gap, 0))).astype(BF16))
    wout, wgu, wdown = w_out.astype(BF16), w_gate_up.astype(BF16), w_down.astype(BF16)
    for l in range(depth):
        bs_pair = jnp.repeat(
            b_s[l].reshape(N_GROUPS // 2, 2, SGU_CHUNK).transpose(0, 2, 1), GROUP_DIM, axis=2)
        q, k, v, c_t, norms, sgu = _proj_call(
            l, xs, row(mix_norm_g[l]), win, b_f[l].reshape(N_HEADS, 1).astype(F32),
            row(sgu_ln_g[l]), row(sgu_ln_b[l]), w_s[l].astype(F32), bs_pair.astype(F32),
            row(out_norm_g[l, d_attn:]))
        tabs = norms[:, :2, :N_HEADS].transpose(1, 2, 0)
        attn = _attn_call(c_t[:, ::ATTN_TILE], tabs[0], tabs[1], q, k, v)
        xs = _post_call(
            l, xs, attn, sgu, row(out_norm_g[l, :d_attn]), wout, row(ffn_norm_g[l]), wgu, wdown,
            row(final_norm_g), final_norm=(l == depth - 1))
    return xs.reshape(batch, seq, d_model).astype(x.dtype)
```

```python
import functools
import math

import jax
import jax.numpy as jnp
from jax import lax
from jax.experimental import pallas as pl
from jax.experimental.pallas import tpu as pltpu

F32 = jnp.float32
BF16 = jnp.bfloat16

EPS = 1e-6
HEAD_DIM = 64
N_HEADS = 8
N_GROUPS = 8
GROUP_DIM = 64
SGU_CHUNK = 128
STREAM_CHUNK = 64
LANES = 128
F_PAD = LANES

PROJ_ROWS = 512
ATTN_TILE = 512
ATTN_TILES_PER_STEP = 8
POST_ROWS = 512
FF_CHUNK = 256
LOG2E = math.log2(math.e)
QK_SCALE = HEAD_DIM ** -0.5 * LOG2E
NORM_SLACK = 1.01
PRUNE_BITS = 40.0
ONES_ROWS = 16
FEAT_STRIDE = 8
VMEM_LIMIT = 56 * 1024 * 1024


def _rms_scale(x):
    return x * lax.rsqrt(jnp.mean(x * x, axis=-1, keepdims=True) + EPS)


def _resident(shape):
    zeros = (0,) * len(shape)
    return pl.BlockSpec(shape, lambda *_: zeros, pipeline_mode=pl.Buffered(1))


def _layer_resident(stacked, layer):
    tail = (0,) * (stacked.ndim - 1)
    return pl.BlockSpec((None,) + stacked.shape[1:], lambda *_: (layer,) + tail,
                        pipeline_mode=pl.Buffered(1))


def _split3(a):
    hi = a.astype(BF16).astype(F32)
    r = a - hi
    mid = r.astype(BF16).astype(F32)
    lo = (r - mid).astype(BF16).astype(F32)
    return hi, mid, lo


def _proj_kernel(x_ref, g_ref, win_ref, bf_ref, lng_ref, lnb_ref, ws_ref, bs_ref,
                 og_ref, q_ref, k_ref, v_ref, ct_ref, nrm_ref, sgu_ref, carry_ref, mix_ref,
                 h_ref, z_ref, zu_ref, zvb_ref):
    rows = x_ref.shape[0]
    d_attn = v_ref.shape[0]
    d_sgu = sgu_ref.shape[1]
    step = pl.program_id(0)
    n_h = h_ref.shape[2]

    @pl.when(step == 0)
    def _():
        h_ref[1] = jnp.zeros(h_ref.shape[1:], F32)
        z_ref[1] = jnp.zeros(z_ref.shape[1:], F32)

    @pl.when(step <= 1)
    def _():
        carry_ref[...] = jnp.zeros_like(carry_ref)

    def work(cur, prev):
        xb = (_rms_scale(x_ref[...]) * g_ref[...]).astype(BF16)
        z_ref[cur] = jnp.dot(xb, win_ref[:, n_h:], preferred_element_type=F32)

        z = z_ref[prev]
        gz = 0.5 * z * (1.0 + lax.erf(z * (2.0 ** -0.5)))
        zv = gz[:, d_sgu:]
        mu = jnp.mean(zv, axis=-1, keepdims=True)
        zc = zv - mu
        zvn = zc * lax.rsqrt(jnp.mean(zc * zc, axis=-1, keepdims=True) + EPS)
        zu_ref[...] = gz[:, :d_sgu]
        zvb_ref[...] = (zvn * lng_ref[...] + lnb_ref[...]).astype(BF16)

        h_ref[cur] = jnp.dot(xb, win_ref[:, :n_h], preferred_element_type=F32)

        h = h_ref[prev]
        for blk in range(d_attn // LANES):
            lo = 2 * d_attn + blk * LANES
            v_ref[blk * LANES:(blk + 1) * LANES, :] = jnp.transpose(h[:, lo:lo + LANES]).astype(BF16)

        f_t = jnp.transpose(h[:, 3 * d_attn:])[:N_HEADS, :] + bf_ref[...]
        log_f = jnp.minimum(f_t, 0.0) - jnp.log1p(jnp.exp(-jnp.abs(f_t)))
        lane = lax.broadcasted_iota(jnp.int32, log_f.shape, 1)
        c = log_f
        shift = 1
        while shift < rows:
            c = c + jnp.where(lane >= shift, pltpu.roll(c, shift, axis=1), 0.0)
            shift *= 2
        c = c + carry_ref[:, 0:1]
        ct_ref[...] = c
        carry_ref[...] = jnp.broadcast_to(c[:, rows - 1:rows], carry_ref.shape)

        a = jnp.concatenate(
            [c[:, u:u + ATTN_TILE] - c[:, u:u + 1] for u in range(0, rows, ATTN_TILE)], axis=1)
        parts = _split3(a * LOG2E)
        sub = lax.broadcasted_iota(jnp.int32, (8, rows), 0)

        def feature_rows(head):
            pick = lambda part: jnp.broadcast_to(part[head:head + 1, :], (8, rows))
            return jnp.where(sub == 0, pick(parts[0]),
                             jnp.where(sub == 1, pick(parts[1]),
                                       jnp.where(sub == 2, pick(parts[2]), 0.0)))

        lane = lax.broadcasted_iota(jnp.int32, (rows, LANES), 1)
        lane1 = lax.broadcasted_iota(jnp.int32, (1, LANES), 1)
        pad = jnp.zeros((HEAD_DIM - 2 * FEAT_STRIDE, rows), F32)
        head_sel = (lax.broadcasted_iota(jnp.int32, (d_attn, LANES), 0) // HEAD_DIM
                    == lax.broadcasted_iota(jnp.int32, (d_attn, LANES), 1)).astype(BF16)

        def max_row_norms(block):
            b16 = block.astype(BF16)
            nsq = jnp.dot(b16 * b16, head_sel, preferred_element_type=F32)
            return jnp.sqrt(jnp.max(nsq, axis=0, keepdims=True) * NORM_SLACK)

        nrm_ref[0, 0:1, :] = max_row_norms(h[:, :d_attn] * QK_SCALE)
        nrm_ref[0, 1:2, :] = max_row_norms(h[:, d_attn:2 * d_attn])
        nrm_ref[0, 2:, :] = jnp.zeros((N_HEADS - 2, LANES), F32)

        for pair in range(N_HEADS // 2):
            f_even, f_odd = feature_rows(2 * pair), feature_rows(2 * pair + 1)
            feats = jnp.transpose(
                jnp.concatenate([f_odd, f_odd, pad, f_even, f_even, pad], axis=0))
            blk = slice(pair * LANES, (pair + 1) * LANES)
            qb = h[:, blk] * QK_SCALE
            kb = h[:, d_attn + pair * LANES:d_attn + (pair + 1) * LANES]
            for odd in (0, 1):
                base = 0 if odd else HEAD_DIM
                own = (lane >= HEAD_DIM) if odd else (lane < HEAD_DIM)
                first = lane < base + FEAT_STRIDE
                in_first = (lane1 >= base) & (lane1 < base + 3)
                in_second = (lane1 >= base + FEAT_STRIDE) & (lane1 < base + FEAT_STRIDE + 3)
                q_aug = jnp.where(own, qb,
                                  jnp.where(first, feats, jnp.where(in_second, -1.0, 0.0)))
                k_aug = jnp.where(own, kb,
                                  jnp.where(first, jnp.where(in_first, 1.0, 0.0), feats))
                dst = slice((2 * pair + odd) * LANES, (2 * pair + odd + 1) * LANES)
                q_ref[:, dst] = q_aug.astype(BF16)
                k_ref[:, dst] = k_aug.astype(BF16)

        ri = lax.broadcasted_iota(jnp.int32, (SGU_CHUNK, SGU_CHUNK), 0) // STREAM_CHUNK
        ci = lax.broadcasted_iota(jnp.int32, (SGU_CHUNK, SGU_CHUNK), 1) // STREAM_CHUNK
        w_mask = (ci <= ri).astype(F32)
        wm = (ws_ref[...] * w_mask[None]).astype(BF16)
        low_half = lax.broadcasted_iota(jnp.int32, (SGU_CHUNK, LANES), 1) < GROUP_DIM
        for pair in range(N_GROUPS // 2):
            w_pair = wm[2 * pair:2 * pair + 2].reshape(2 * SGU_CHUNK, SGU_CHUNK)
            cols = slice(pair * LANES, (pair + 1) * LANES)
            for chunk in range(rows // SGU_CHUNK):
                rws = slice(chunk * SGU_CHUNK, (chunk + 1) * SGU_CHUNK)
                both = jnp.dot(w_pair, zvb_ref[rws, cols], preferred_element_type=F32)
                mixed = jnp.where(low_half, both[:SGU_CHUNK], both[SGU_CHUNK:]) + bs_ref[pair]
                mix_ref[rws, cols] = zu_ref[rws, cols] * mixed
        sgu_ref[...] = (_rms_scale(mix_ref[...]) * og_ref[...]).astype(BF16)

    pl.when(step % 2 == 0)(lambda: work(0, 1))
    pl.when(step % 2 == 1)(lambda: work(1, 0))


def _proj_call(layer, x, g, win, bf, lng, lnb, ws, bs_pair, og):
    s, d = x.shape
    d_attn, d_sgu = N_HEADS * HEAD_DIM, N_GROUPS * GROUP_DIM
    n_h = 3 * d_attn + F_PAD
    rows = PROJ_ROWS
    tiles = s // rows
    done = lambda i: jnp.maximum(i - 1, 0)
    row_blk = lambda w: pl.BlockSpec((rows, w), lambda i: (done(i), 0))
    return pl.pallas_call(
        _proj_kernel,
        grid=(tiles + 1,),
        in_specs=[pl.BlockSpec((rows, d), lambda i: (jnp.minimum(i, tiles - 1), 0)),
                  _resident(g.shape), _layer_resident(win, layer), _resident(bf.shape),
                  _resident(lng.shape), _resident(lnb.shape),
                  _resident(ws.shape), _resident(bs_pair.shape), _resident(og.shape)],
        out_specs=[row_blk(N_HEADS * LANES), row_blk(N_HEADS * LANES),
                   pl.BlockSpec((d_attn, rows), lambda i: (0, done(i))),
                   pl.BlockSpec((N_HEADS, rows), lambda i: (0, done(i))),
                   pl.BlockSpec((1, N_HEADS, LANES), lambda i: (done(i), 0, 0)), row_blk(d_sgu)],
        out_shape=[jax.ShapeDtypeStruct((s, N_HEADS * LANES), BF16)] * 2
        + [jax.ShapeDtypeStruct((d_attn, s), BF16),
           jax.ShapeDtypeStruct((N_HEADS, s), F32),
           jax.ShapeDtypeStruct((tiles, N_HEADS, LANES), F32),
           jax.ShapeDtypeStruct((s, d_sgu), BF16)],
        scratch_shapes=[pltpu.VMEM((N_HEADS, LANES), F32), pltpu.VMEM((rows, d_sgu), F32),
                        pltpu.VMEM((2, rows, n_h), F32),
                        pltpu.VMEM((2, rows, 2 * d_sgu), F32),
                        pltpu.VMEM((rows, d_sgu), F32), pltpu.VMEM((rows, d_sgu), BF16)],
        compiler_params=pltpu.CompilerParams(
            dimension_semantics=("arbitrary",), vmem_limit_bytes=VMEM_LIMIT),
        name="fox_proj",
    )(x, g, win, bf, lng, lnb, ws, bs_pair, og)


def _attn_kernel(ctab_ref, qn_ref, kn_ref, q_ref, k_ref, v_ref, o_ref, m_ref, acc_ref, s_ref,
                 smax_ref, pk_ref):
    t = ATTN_TILE
    pair = pl.program_id(0)
    lax.fori_loop(0, q_ref.shape[0] // t, functools.partial(
        _attn_query_tile, ctab_ref, qn_ref, kn_ref, q_ref, k_ref, v_ref, o_ref, m_ref, acc_ref,
        s_ref, smax_ref, pk_ref, pair, pl.program_id(1) * (q_ref.shape[0] // t)), 0)


def _attn_query_tile(ctab_ref, qn_ref, kn_ref, q_ref, k_ref, v_ref, o_ref, m_ref, acc_ref, s_ref,
                     smax_ref, pk_ref, pair, first_tile, sub, carry):
    t = ATTN_TILE
    i = first_tile + sub
    q_rows = pl.ds(pl.multiple_of(sub * t, t), t)

    key = lax.broadcasted_iota(jnp.int32, (t, t), 0)
    qry = lax.broadcasted_iota(jnp.int32, (t, t), 1)
    ones = jnp.ones((ONES_ROWS, t), BF16)
    nt_dims = (((1,), (1,)), ((), ()))

    def scores_t(hh, start):
        hl = slice(hh * LANES, (hh + 1) * LANES)
        return lax.dot_general(k_ref[pl.ds(start, t), hl], q_ref[q_rows, hl], nt_dims,
                               preferred_element_type=F32)

    def weighted_values_t(hh, p, start):
        v_t = v_ref[hh * HEAD_DIM:(hh + 1) * HEAD_DIM, pl.ds(start, t)]
        v_ext = jnp.concatenate([v_t, ones], axis=0)
        return jnp.dot(v_ext, p.astype(BF16), preferred_element_type=F32)

    def tile_start(j):
        return pl.multiple_of(jnp.maximum(j, 0) * t, t)

    def put_scores(hh, s, diagonal):
        if diagonal:
            s = jnp.where(key <= qry, s, -jnp.inf)
        s_ref[hh] = s
        smax_ref[hh] = jnp.max(s, axis=0, keepdims=True)

    def weights(hh, j):
        head = 2 * pair + hh
        shift = (ctab_ref[head, i] - ctab_ref[head, j]) * LOG2E
        m_old = m_ref[hh]
        m_new = jnp.maximum(m_old, smax_ref[hh] + shift)
        m_ref[hh] = m_new
        return jnp.exp2(s_ref[hh] - (m_new - shift)), jnp.exp2(m_old - m_new)

    def accumulate(hh, j, p, alpha):
        acc_ref[hh] = alpha * acc_ref[hh] + weighted_values_t(hh, p, tile_start(j))

    def step(j, diagonal):
        put_scores(1, scores_t(1, tile_start(j)), diagonal)
        p0, alpha0 = weights(0, j)
        s_next = scores_t(0, tile_start(j - 1))
        accumulate(0, j, p0, alpha0)
        put_scores(0, s_next, False)
        p1, alpha1 = weights(1, j)
        accumulate(1, j, p1, alpha1)

    for hh in range(2):
        def scan(j, running, hh=hh):
            running = jnp.maximum(running, kn_ref[2 * pair + hh, j])
            pk_ref[hh, j] = running
            return running
        lax.fori_loop(0, i + 1, scan, 0.0)

    m_ref[...] = jnp.full(m_ref.shape, -jnp.inf, F32)
    acc_ref[...] = jnp.zeros(acc_ref.shape, F32)
    put_scores(0, scores_t(0, tile_start(i)), True)
    step(i, True)

    m_low = jnp.min(m_ref[...], axis=-1)
    floors = [m_low[hh, 0] - PRUNE_BITS for hh in range(2)]

    def needed(j):
        jc = jnp.maximum(j, 0)
        return functools.reduce(jnp.logical_or, [
            qn_ref[2 * pair + hh, i] * pk_ref[hh, jc]
            + (ctab_ref[2 * pair + hh, i] - ctab_ref[2 * pair + hh, jc + 1]) * LOG2E > floors[hh]
            for hh in range(2)])

    j_lo = lax.while_loop(lambda j: jnp.logical_and(j > 0, needed(j - 1)), lambda j: j - 1, i)
    lax.fori_loop(0, i - j_lo, lambda u, carry: (step(i - 1 - u, False), carry)[1], 0)

    out_t = jnp.concatenate(
        [acc_ref[hh, :HEAD_DIM, :] / acc_ref[hh, HEAD_DIM:HEAD_DIM + 1, :] for hh in range(2)],
        axis=0)
    o_ref[q_rows, :] = jnp.transpose(out_t)
    return carry


def _attn_call(ctab, qn, kn, q, k, v):
    d_attn, s = v.shape
    t = ATTN_TILE
    tq = t * ATTN_TILES_PER_STEP
    pairs = d_attn // LANES
    return pl.pallas_call(
        _attn_kernel,
        grid=(pairs, s // tq),
        in_specs=[pl.BlockSpec(memory_space=pltpu.SMEM)] * 3
                 + [pl.BlockSpec((tq, 2 * LANES), lambda hp, i: (i, hp)),
                    pl.BlockSpec((s, 2 * LANES), lambda hp, i: (0, hp)),
                    pl.BlockSpec((LANES, s), lambda hp, i: (hp, 0))],
        out_specs=pl.BlockSpec((tq, LANES), lambda hp, i: (i, hp)),
        out_shape=jax.ShapeDtypeStruct((s, d_attn), F32),
        scratch_shapes=[pltpu.VMEM((2, 1, t), F32),
                        pltpu.VMEM((2, HEAD_DIM + ONES_ROWS, t), F32),
                        pltpu.VMEM((2, t, t), F32), pltpu.VMEM((2, 1, t), F32),
                        pltpu.SMEM((2, s // t), F32)],
        compiler_params=pltpu.CompilerParams(
            dimension_semantics=("arbitrary", "arbitrary"), vmem_limit_bytes=VMEM_LIMIT),
        name="fox_attn",
    )(ctab, qn, kn, q, k, v)


def _post_kernel(x_ref, attn_ref, sgu_ref, oga_ref, wout_ref, fg_ref, wgu_ref, wdown_ref,
                 fin_ref, o_ref, act_ref, *, final_norm):
    d_ff = wdown_ref.shape[0]
    an = (_rms_scale(attn_ref[...]) * oga_ref[...]).astype(BF16)
    merged = jnp.concatenate([an, sgu_ref[...]], axis=-1)
    x1 = x_ref[...] + jnp.dot(merged, wout_ref[...], preferred_element_type=F32)

    xb = (_rms_scale(x1) * fg_ref[...]).astype(BF16)
    for c in range(d_ff // FF_CHUNK):
        lo = c * FF_CHUNK
        gate = jnp.dot(xb, wgu_ref[:, lo:lo + FF_CHUNK], preferred_element_type=F32)
        up = jnp.dot(xb, wgu_ref[:, d_ff + lo:d_ff + lo + FF_CHUNK], preferred_element_type=F32)
        act_ref[:, lo:lo + FF_CHUNK] = (gate * jax.nn.sigmoid(gate) * up).astype(BF16)
    x2 = x1 + jnp.dot(act_ref[...], wdown_ref[...], preferred_element_type=F32)
    if final_norm:
        x2 = _rms_scale(x2) * fin_ref[...]
    o_ref[...] = x2


def _post_call(layer, x, attn, sgu, oga, wout, fg, wgu, wdown, fin, final_norm):
    s, d = x.shape
    rows = POST_ROWS
    d_ff = wdown.shape[1]
    row_blk = lambda w: pl.BlockSpec((rows, w), lambda i: (i, 0))
    return pl.pallas_call(
        functools.partial(_post_kernel, final_norm=final_norm),
        grid=(s // rows,),
        in_specs=[row_blk(d), row_blk(attn.shape[1]), row_blk(sgu.shape[1]),
                  _resident(oga.shape), _layer_resident(wout, layer), _resident(fg.shape),
                  _layer_resident(wgu, layer), _layer_resident(wdown, layer),
                  _resident(fin.shape)],
        out_specs=row_blk(d),
        out_shape=jax.ShapeDtypeStruct((s, d), F32),
        scratch_shapes=[pltpu.VMEM((rows, d_ff), BF16)],
        compiler_params=pltpu.CompilerParams(
            dimension_semantics=("arbitrary",), vmem_limit_bytes=VMEM_LIMIT),
        name="fox_post",
    )(x, attn, sgu, oga, wout, fg, wgu, wdown, fin)


def kernel(x, mix_norm_g, w_in, b_f, sgu_ln_g, sgu_ln_b, w_s, b_s, out_norm_g, w_out,
           ffn_norm_g, w_gate_up, w_down, final_norm_g):
    batch, seq, d_model = x.shape
    depth = w_in.shape[0]
    d_attn = N_HEADS * HEAD_DIM
    d_sgu = N_GROUPS * GROUP_DIM
    assert batch == 1 and w_in.shape[2] == 3 * d_attn + N_HEADS + 2 * d_sgu
    assert seq % max(PROJ_ROWS, POST_ROWS, ATTN_TILE * ATTN_TILES_PER_STEP) == 0
    assert PROJ_ROWS == ATTN_TILE
    assert w_down.shape[1] % FF_CHUNK == 0 and w_s.shape[2] == SGU_CHUNK

    row = lambda a: a.reshape(1, -1).astype(F32)
    xs = x.reshape(seq, d_model).astype(F32)
    n_qkvf = 3 * d_attn + N_HEADS
    gap = F_PAD - N_HEADS
    win = (jnp.pad(w_in[:, :, :n_qkvf], ((0, 0), (0, 0), (0, gap + 2 * d_sgu))).astype(BF16)
           + jnp.pad(w_in[:, :, n_qkvf:], ((0, 0), (0, 0), (n_qkvf + gap, 0))).astype(BF16))
    wout, wgu, wdown = w_out.astype(BF16), w_gate_up.astype(BF16), w_down.astype(BF16)
    for l in range(depth):
        bs_pair = jnp.repeat(
            b_s[l].reshape(N_GROUPS // 2, 2, SGU_CHUNK).transpose(0, 2, 1), GROUP_DIM, axis=2)
        q, k, v, c_t, norms, sgu = _proj_call(
            l, xs, row(mix_norm_g[l]), win, b_f[l].reshape(N_HEADS, 1).astype(F32),
            row(sgu_ln_g[l]), row(sgu_ln_b[l]), w_s[l].astype(F32), bs_pair.astype(F32),
            row(out_norm_g[l, d_attn:]))
        tabs = norms[:, :2, :N_HEADS].transpose(1, 2, 0)
        attn = _attn_call(c_t[:, ::ATTN_TILE], tabs[0], tabs[1], q, k, v)
        xs = _post_call(
            l, xs, attn, sgu, row(out_norm_g[l, :d_attn]), wout, row(ffn_norm_g[l]), wgu, wdown,
            row(final_norm_g), final_norm=(l == depth - 1))
    return xs.reshape(batch, seq, d_model).astype(x.dtype)
```
